```python
import math
import jax, jax.numpy as jnp
from jax import lax
import numpy as np

D_MODEL = 1024
BATCH = 2
SEQ = 8192
DEPTH = 4
DEC_BATCH = 128
DEC_SEQ = 1
PAST_LEN = 8192
PAGE_SIZE = 128

CHUNK = 128
SGU_GROUP = 128
D_SGU = 2 * D_MODEL
N_SGU_GROUPS = D_SGU // SGU_GROUP
N_HEADS = 8
QK_NOPE = 128
QK_ROPE = 64
V_HEAD = 128
Q_LORA = 384
KV_LORA = 256
ROPE_THETA = 10000.0
Q_BLOCK = 128
ATTN_SCALE = 1.0 / math.sqrt(QK_NOPE + QK_ROPE)
D_FF = 5 * D_MODEL // 2
N_EXPERTS = 8
TOP_K = 2
D_FF_EXPERT = 7 * D_MODEL // 2
NORM_EPS = 1e-6
LN_EPS = 1e-5
NEG_INF = -1e30

N_A_LAYERS = (DEPTH + 1) // 2
N_B_LAYERS = DEPTH // 2

kernel_name = "hybrid_chunkmlp_mla_moe_step"


def rms_norm(x, g):
    xf = x.astype(jnp.float32)
    y = xf * lax.rsqrt(jnp.mean(xf * xf, axis=-1, keepdims=True) + NORM_EPS)
    return (y * g.astype(jnp.float32)).astype(x.dtype)


def layer_norm(x, g, b):
    xf = x.astype(jnp.float32)
    xc = xf - jnp.mean(xf, axis=-1, keepdims=True)
    y = xc * lax.rsqrt(jnp.mean(xc * xc, axis=-1, keepdims=True) + LN_EPS)
    return (y * g.astype(jnp.float32) + b.astype(jnp.float32)).astype(x.dtype)


def rope(x, pos):
    half = x.shape[-1] // 2
    inv_freq = ROPE_THETA ** (-jnp.arange(half, dtype=jnp.float32) / half)
    ang = pos.astype(jnp.float32)[:, None] * inv_freq[None, :]
    cos = jnp.cos(ang)[:, None, :]
    sin = jnp.sin(ang)[:, None, :]
    xf = x.astype(jnp.float32)
    x1, x2 = xf[..., :half], xf[..., half:]
    return jnp.concatenate([x1 * cos - x2 * sin, x1 * sin + x2 * cos], axis=-1).astype(x.dtype)


def swiglu(h, w_gate, w_up, w_down):
    return (jax.nn.silu(h @ w_gate) * (h @ w_up)) @ w_down


def chunk_mlp(h, w_in, ln_g, ln_b, w_s, b_s, w_out):
    n, t, _ = h.shape
    blk = min(t, CHUNK)
    z = jax.nn.gelu(h @ w_in, approximate=False)
    u, v = z[..., :D_SGU], z[..., D_SGU:]
    v = layer_norm(v, ln_g, ln_b)
    vg = v.reshape(n, t // blk, blk, N_SGU_GROUPS, SGU_GROUP)
    causal = jnp.tril(jnp.ones((blk, blk), dtype=bool))
    ws = jnp.where(causal[None], w_s[:, :blk, :blk], 0.0).astype(v.dtype)
    bias = b_s[:, :blk].T[None, None, :, :, None].astype(v.dtype)
    mixed = jnp.einsum('gij,ncjgd->ncigd', ws, vg) + bias
    gated = u * mixed.reshape(n, t, D_SGU)
    return gated @ w_out, v


def mla_project(h, pos, w_dq, q_norm_g, w_uq, w_dkv, kv_norm_g):
    c_q = rms_norm(h @ w_dq, q_norm_g)
    q = jnp.einsum('ntr,rhe->nthe', c_q, w_uq)
    q_nope = q[..., :QK_NOPE]
    q_rope = rope(q[..., QK_NOPE:], pos)
    ckv = h @ w_dkv
    c_kv = rms_norm(ckv[..., :KV_LORA], kv_norm_g)
    k_rope = rope(ckv[..., None, KV_LORA:], pos)[..., 0, :]
    return q_nope, q_rope, c_kv, k_rope


def mla_prompt(h, pos, w_dq, q_norm_g, w_uq, w_dkv, kv_norm_g, w_uk, w_uv, w_o):
    n, t, _ = h.shape
    q_nope, q_rope, c_kv, k_rope = mla_project(h, pos, w_dq, q_norm_g, w_uq, w_dkv, kv_norm_g)
    k_nope = jnp.einsum('nkr,rhe->nkhe', c_kv, w_uk)
    v = jnp.einsum('nkr,rhe->nkhe', c_kv, w_uv)
    nb = t // Q_BLOCK
    qn_b = q_nope.reshape(n, nb, Q_BLOCK, N_HEADS, QK_NOPE).transpose(1, 0, 2, 3, 4)
    qr_b = q_rope.reshape(n, nb, Q_BLOCK, N_HEADS, QK_ROPE).transpose(1, 0, 2, 3, 4)
    qpos_b = pos.reshape(nb, Q_BLOCK)

    def attend(blk):
        qn, qr, qp = blk
        s = (jnp.einsum('nqhe,nkhe->nhqk', qn, k_nope)
             + jnp.einsum('nqhe,nke->nhqk', qr, k_rope)).astype(jnp.float32) * ATTN_SCALE
        s = jnp.where(pos[None, :] <= qp[:, None], s, NEG_INF)
        p = jax.nn.softmax(s, axis=-1).astype(v.dtype)
        return jnp.einsum('nhqk,nkhe->nqhe', p, v)

    o = lax.map(attend, (qn_b, qr_b, qpos_b))
    o = o.transpose(1, 0, 2, 3, 4).reshape(n, t, N_HEADS * V_HEAD)
    return o @ w_o, c_kv, k_rope


def mla_sample(h, pos, lat_pool, kr_pool, page_table, w_dq, q_norm_g, w_uq, w_dkv, kv_norm_g, w_uk, w_uv, w_o):
    n, t, _ = h.shape
    q_nope, q_rope, c_new, kr_new = mla_project(h, pos, w_dq, q_norm_g, w_uq, w_dkv, kv_norm_g)
    lat_past = lat_pool[page_table].reshape(n, -1, KV_LORA)
    kr_past = kr_pool[page_table].reshape(n, -1, QK_ROPE)
    n_past = lat_past.shape[1]
    q_lat = jnp.einsum('nthe,rhe->nthr', q_nope, w_uk)
    s_past = (jnp.einsum('nthr,nkr->nhtk', q_lat, lat_past)
              + jnp.einsum('nthe,nke->nhtk', q_rope, kr_past)).astype(jnp.float32) * ATTN_SCALE
    s_new = (jnp.einsum('nthr,nkr->nhtk', q_lat, c_new)
             + jnp.einsum('nthe,nke->nhtk', q_rope, kr_new)).astype(jnp.float32) * ATTN_SCALE
    s_new = jnp.where(jnp.tril(jnp.ones((t, t), dtype=bool)), s_new, NEG_INF)
    p = jax.nn.softmax(jnp.concatenate([s_past, s_new], axis=-1), axis=-1).astype(h.dtype)
    o_lat = (jnp.einsum('nhtk,nkr->nthr', p[..., :n_past], lat_past)
             + jnp.einsum('nhtk,nkr->nthr', p[..., n_past:], c_new))
    o = jnp.einsum('nthr,rhe->nthe', o_lat, w_uv).reshape(n, t, N_HEADS * V_HEAD)
    return o @ w_o, c_new, kr_new


def moe_ffn(h, w_router, w_gate, w_up, w_down):
    logits = (h @ w_router).astype(jnp.float32)
    top_v, top_i = lax.top_k(logits, TOP_K)
    g = jax.nn.softmax(top_v, axis=-1)
    combine = jnp.sum(jax.nn.one_hot(top_i, N_EXPERTS, dtype=jnp.float32) * g[..., None], axis=-2).astype(h.dtype)
    y = jnp.zeros_like(h)
    for e in range(N_EXPERTS):
        y = y + combine[..., e:e + 1] * swiglu(h, w_gate[e], w_up[e], w_down[e])
    return y


def setup_inputs(seed: int = 0) -> dict:
    key = jax.random.key(seed)
    ks = iter(jax.random.split(key, 40))

    def nrm(shape, scale):
        return scale * jax.random.normal(next(ks), shape, jnp.float32)

    n_pages = PAST_LEN // PAGE_SIZE
    n_used = DEC_BATCH * n_pages
    n_pool = n_used + (n_used + 3) // 4
    x_prompt = nrm((BATCH, SEQ, D_MODEL), 1.0)
    x_sample = nrm((DEC_BATCH, DEC_SEQ, D_MODEL), 1.0)
    cache_latent = nrm((N_B_LAYERS, n_pool, PAGE_SIZE, KV_LORA), 1.0)
    cache_k_rope = nrm((N_B_LAYERS, n_pool, PAGE_SIZE, QK_ROPE), 1.0)
    perm = jax.random.permutation(next(ks), n_pool)
    page_table = perm[:n_used].reshape(DEC_BATCH, n_pages).astype(jnp.int32)
    d = D_MODEL
    return {
        'x_prompt': x_prompt,
        'x_sample': x_sample,
        'cache_latent': cache_latent,
        'cache_k_rope': cache_k_rope,
        'page_table': page_table,
        'norm_mix_g': 1.0 + nrm((DEPTH, d), 0.02),
        'norm_ffn_g': 1.0 + nrm((DEPTH, d), 0.02),
        'final_norm_g': 1.0 + nrm((d,), 0.02),
        'sgu_w_in': nrm((N_A_LAYERS, d, 2 * D_SGU), d ** -0.5),
        'sgu_ln_g': 1.0 + nrm((N_A_LAYERS, D_SGU), 0.02),
        'sgu_ln_b': nrm((N_A_LAYERS, D_SGU), 0.02),
        'sgu_w_s': nrm((N_A_LAYERS, N_SGU_GROUPS, CHUNK, CHUNK), CHUNK ** -0.5),
        'sgu_b_s': 1.0 + nrm((N_A_LAYERS, N_SGU_GROUPS, CHUNK), 0.02),
        'sgu_w_out': nrm((N_A_LAYERS, D_SGU, d), D_SGU ** -0.5),
        'mla_w_dq': nrm((N_B_LAYERS, d, Q_LORA), d ** -0.5),
        'mla_q_norm_g': 1.0 + nrm((N_B_LAYERS, Q_LORA), 0.02),
        'mla_w_uq': nrm((N_B_LAYERS, Q_LORA, N_HEADS, QK_NOPE + QK_ROPE), Q_LORA ** -0.5),
        'mla_w_dkv': nrm((N_B_LAYERS, d, KV_LORA + QK_ROPE), d ** -0.5),
        'mla_kv_norm_g': 1.0 + nrm((N_B_LAYERS, KV_LORA), 0.02),
        'mla_w_uk': nrm((N_B_LAYERS, KV_LORA, N_HEADS, QK_NOPE), KV_LORA ** -0.5),
        'mla_w_uv': nrm((N_B_LAYERS, KV_LORA, N_HEADS, V_HEAD), KV_LORA ** -0.5),
        'mla_w_o': nrm((N_B_LAYERS, N_HEADS * V_HEAD, d), (N_HEADS * V_HEAD) ** -0.5),
        'ffn_w_gate': nrm((N_A_LAYERS, d, D_FF), d ** -0.5),
        'ffn_w_up': nrm((N_A_LAYERS, d, D_FF), d ** -0.5),
        'ffn_w_down': nrm((N_A_LAYERS, D_FF, d), D_FF ** -0.5),
        'moe_w_router': nrm((N_B_LAYERS, d, N_EXPERTS), d ** -0.5),
        'moe_w_gate': nrm((N_B_LAYERS, N_EXPERTS, d, D_FF_EXPERT), d ** -0.5),
        'moe_w_up': nrm((N_B_LAYERS, N_EXPERTS, d, D_FF_EXPERT), d ** -0.5),
        'moe_w_down': nrm((N_B_LAYERS, N_EXPERTS, D_FF_EXPERT, d), D_FF_EXPERT ** -0.5),
    }


def reference(x_prompt, x_sample, cache_latent, cache_k_rope, page_table,
              norm_mix_g, norm_ffn_g, final_norm_g,
              sgu_w_in, sgu_ln_g, sgu_ln_b, sgu_w_s, sgu_b_s, sgu_w_out,
              mla_w_dq, mla_q_norm_g, mla_w_uq, mla_w_dkv, mla_kv_norm_g, mla_w_uk, mla_w_uv, mla_w_o,
              ffn_w_gate, ffn_w_up, ffn_w_down,
              moe_w_router, moe_w_gate, moe_w_up, moe_w_down):
    pos_p = jnp.arange(SEQ, dtype=jnp.int32)
    pos_s = PAST_LEN + jnp.arange(DEC_SEQ, dtype=jnp.int32)
    xp, xs = x_prompt, x_sample
    lat_p, kr_p, lat_s, kr_s, v_s = [], [], [], [], []
    for i in range(DEPTH):
        j = i // 2
        hp = rms_norm(xp, norm_mix_g[i])
        hs = rms_norm(xs, norm_mix_g[i])
        if i % 2 == 0:
            sgu = (sgu_w_in[j], sgu_ln_g[j], sgu_ln_b[j], sgu_w_s[j], sgu_b_s[j], sgu_w_out[j])
            op, _ = chunk_mlp(hp, *sgu)
            os_, vs = chunk_mlp(hs, *sgu)
            v_s.append(vs)
        else:
            mla = (mla_w_dq[j], mla_q_norm_g[j], mla_w_uq[j], mla_w_dkv[j], mla_kv_norm_g[j],
                   mla_w_uk[j], mla_w_uv[j], mla_w_o[j])
            op, cp, rp = mla_prompt(hp, pos_p, *mla)
            os_, cs, rs = mla_sample(hs, pos_s, cache_latent[j], cache_k_rope[j], page_table, *mla)
            lat_p.append(cp)
            kr_p.append(rp)
            lat_s.append(cs)
            kr_s.append(rs)
        xp = xp + op
        xs = xs + os_
        hp = rms_norm(xp, norm_ffn_g[i])
        hs = rms_norm(xs, norm_ffn_g[i])
        if i % 2 == 0:
            xp = xp + swiglu(hp, ffn_w_gate[j], ffn_w_up[j], ffn_w_down[j])
            xs = xs + swiglu(hs, ffn_w_gate[j], ffn_w_up[j], ffn_w_down[j])
        else:
            xp = xp + moe_ffn(hp, moe_w_router[j], moe_w_gate[j], moe_w_up[j], moe_w_down[j])
            xs = xs + moe_ffn(hs, moe_w_router[j], moe_w_gate[j], moe_w_up[j], moe_w_down[j])
    y_prompt = rms_norm(xp, final_norm_g)
    y_sample = rms_norm(xs, final_norm_g)
    return (y_prompt, y_sample, jnp.stack(lat_p), jnp.stack(kr_p), jnp.stack(lat_s), jnp.stack(kr_s), jnp.stack(v_s))
```

```python
import functools
import math

import jax
import jax.numpy as jnp
from jax import lax
from jax.experimental import pallas as pl
from jax.experimental.pallas import tpu as pltpu

F32 = jnp.float32
BF16 = jnp.bfloat16

D_MODEL = 1024
BATCH = 2
SEQ = 8192
DEPTH = 4
N_SAMPLE = 128
PAST_LEN = 8192
PAGE_SIZE = 128
N_PAGES = PAST_LEN // PAGE_SIZE

CHUNK = 128
D_SGU = 2 * D_MODEL
N_SGU_GROUPS = D_SGU // CHUNK

N_HEADS = 8
QK_NOPE = 128
QK_ROPE = 64
V_HEAD = 128
Q_LORA = 384
KV_LORA = 256
ROPE_THETA = 10000.0
ATTN_SCALE = 1.0 / math.sqrt(QK_NOPE + QK_ROPE)
QK_PAD = 256

D_FF = 5 * D_MODEL // 2
N_EXPERTS = 8
D_FF_EXPERT = 7 * D_MODEL // 2
NORM_EPS = 1e-6
LN_EPS = 1e-5
NEG_INF = -1e30

TOKEN_BLOCK = 512
N_PROMPT = BATCH * SEQ
N_TOK = N_PROMPT + N_SAMPLE
N_BLOCKS = pl.cdiv(N_TOK, TOKEN_BLOCK)
T_PAD = N_BLOCKS * TOKEN_BLOCK
SAMPLE_BLOCK = N_PROMPT // TOKEN_BLOCK
CHUNKS_PER_BLOCK = TOKEN_BLOCK // CHUNK

ATTN_TQ = 512
ATTN_TK = 512

MOE_TM = 256
MOE_TILES = 2 * N_TOK // MOE_TM + N_EXPERTS
MOE_ROWS = MOE_TILES * MOE_TM
MOE_FC = 512
MOE_TRASH_ROWS = 2048
Y_ROWS = 2 * T_PAD + MOE_TRASH_ROWS

VMEM_LIMIT = 56 * 1024 * 1024


def _params(**kw):
    return pltpu.CompilerParams(vmem_limit_bytes=VMEM_LIMIT, **kw)


def _resident(shape):
    return pl.BlockSpec(shape, lambda *_: (0,) * len(shape), pipeline_mode=pl.Buffered(1))


def _rms_norm(x, g):
    return x * lax.rsqrt(jnp.mean(x * x, axis=-1, keepdims=True) + NORM_EPS) * g


def _gelu(x):
    return 0.5 * x * (1.0 + lax.erf(x * math.sqrt(0.5)))


def _silu(x):
    return x * (1.0 / (1.0 + jnp.exp(-x)))


def _dot(a, b):
    return jnp.dot(a, b, preferred_element_type=F32)


def _split_bf16(w):
    bits = lax.bitcast_convert_type(w, jnp.uint32)
    bits = (bits + jnp.uint32(0x7FFF) + ((bits >> 16) & jnp.uint32(1))) & jnp.uint32(0xFFFF0000)
    hi = lax.bitcast_convert_type(bits, F32)
    return hi.astype(BF16), (w - hi).astype(BF16)


def _dot_split(a, w_hi_ref, w_lo_ref):
    a_hi, a_lo = _split_bf16(a)
    w_hi = w_hi_ref[...]
    return _dot(a_hi, w_hi) + (_dot(a_lo, w_hi) + _dot(a_hi, w_lo_ref[...]))


def _dot_nt(a, b):
    return lax.dot_general(a, b, (((1,), (1,)), ((), ())), preferred_element_type=F32)


def _sgu_gelu_ln(z, lng_ref, lnb_ref):
    z = _gelu(z)
    u = z[:, :D_SGU]
    v = z[:, D_SGU:]
    vc = v - jnp.mean(v, axis=-1, keepdims=True)
    vn = vc * lax.rsqrt(jnp.mean(vc * vc, axis=-1, keepdims=True) + LN_EPS)
    return u, vn * lng_ref[...] + lnb_ref[...]


def _sgu_kernel(x_ref, g_ref, win_ref, winlo_ref, lng_ref, lnb_ref, ws_ref, bias_ref, ws0_ref, bias0_ref,
                wout_ref, woutlo_ref, o_ref, v_ref, gated_ref):
    is_sample = pl.program_id(0) == SAMPLE_BLOCK

    @pl.when(is_sample)
    def _():
        x = x_ref[0:N_SAMPLE, :]
        h = _rms_norm(x, g_ref[...])
        u, vn = _sgu_gelu_ln(_dot_split(h, win_ref, winlo_ref), lng_ref, lnb_ref)
        v_ref[...] = vn
        gated = u * (vn * ws0_ref[...] + bias0_ref[...])
        o_ref[0:N_SAMPLE, :] = x + _dot_split(gated, wout_ref, woutlo_ref)
        o_ref[N_SAMPLE:, :] = jnp.zeros((TOKEN_BLOCK - N_SAMPLE, D_MODEL), F32)

    @pl.when(jnp.logical_not(is_sample))
    def _():
        x = x_ref[...]
        h = _rms_norm(x, g_ref[...]).astype(BF16)
        u, vn = _sgu_gelu_ln(_dot(h, win_ref[...]), lng_ref, lnb_ref)
        vb = vn.astype(BF16)
        for g in range(N_SGU_GROUPS):
            cols = slice(g * CHUNK, (g + 1) * CHUNK)
            rhs = jnp.concatenate(
                [vb[c * CHUNK:(c + 1) * CHUNK, cols] for c in range(CHUNKS_PER_BLOCK)], axis=1)
            mixed = _dot(ws_ref[g], rhs)
            for c in range(CHUNKS_PER_BLOCK):
                rows = slice(c * CHUNK, (c + 1) * CHUNK)
                m = mixed[:, c * CHUNK:(c + 1) * CHUNK] + bias_ref[:, cols]
                gated_ref[rows, cols] = (u[rows, cols] * m).astype(BF16)
        o_ref[...] = x + _dot(gated_ref[...], wout_ref[...])


def _sgu_layer(x, norm_g, w_in, ln_g, ln_b, ws_causal, bias_full, ws0_row, bias0_row, w_out):
    w_in, w_in_lo = _split_bf16(w_in)
    w_out, w_out_lo = _split_bf16(w_out)
    tb = TOKEN_BLOCK
    return pl.pallas_call(
        _sgu_kernel,
        grid=(N_BLOCKS,),
        in_specs=[
            pl.BlockSpec((tb, D_MODEL), lambda i: (i, 0)),
            _resident((1, D_MODEL)),
            _resident((D_MODEL, 2 * D_SGU)),
            _resident((D_MODEL, 2 * D_SGU)),
            _resident((1, D_SGU)),
            _resident((1, D_SGU)),
            _resident((N_SGU_GROUPS, CHUNK, CHUNK)),
            _resident((CHUNK, D_SGU)),
            _resident((1, D_SGU)),
            _resident((1, D_SGU)),
            _resident((D_SGU, D_MODEL)),
            _resident((D_SGU, D_MODEL)),
        ],
        out_specs=[
            pl.BlockSpec((tb, D_MODEL), lambda i: (i, 0)),
            pl.BlockSpec((N_SAMPLE, D_SGU), lambda i: (0, 0)),
        ],
        out_shape=[
            jax.ShapeDtypeStruct((T_PAD, D_MODEL), F32),
            jax.ShapeDtypeStruct((N_SAMPLE, D_SGU), F32),
        ],
        scratch_shapes=[pltpu.VMEM((tb, D_SGU), BF16)],
        compiler_params=_params(),
        name="sgu_mixer",
    )(x, norm_g, w_in, w_in_lo, ln_g, ln_b, ws_causal, bias_full, ws0_row, bias0_row, w_out, w_out_lo)


def _ffn_kernel(x_ref, g_ref, wgu_ref, wgulo_ref, wd_ref, wdlo_ref, o_ref):
    is_sample = pl.program_id(0) == SAMPLE_BLOCK

    @pl.when(is_sample)
    def _():
        x = x_ref[0:N_SAMPLE, :]
        gu = _dot_split(_rms_norm(x, g_ref[...]), wgu_ref, wgulo_ref)
        a = _silu(gu[:, :D_FF]) * gu[:, D_FF:]
        o_ref[0:N_SAMPLE, :] = x + _dot_split(a, wd_ref, wdlo_ref)
        o_ref[N_SAMPLE:, :] = jnp.zeros((TOKEN_BLOCK - N_SAMPLE, D_MODEL), F32)

    @pl.when(jnp.logical_not(is_sample))
    def _():
        x = x_ref[...]
        h = _rms_norm(x, g_ref[...]).astype(BF16)
        gu = _dot(h, wgu_ref[...])
        a = (_silu(gu[:, :D_FF]) * gu[:, D_FF:]).astype(BF16)
        o_ref[...] = x + _dot(a, wd_ref[...])


def _ffn_layer(x, norm_g, w_gu, w_down):
    tb = TOKEN_BLOCK
    w_gu, w_gu_lo = _split_bf16(w_gu)
    w_down, w_down_lo = _split_bf16(w_down)
    return pl.pallas_call(
        _ffn_kernel,
        grid=(N_BLOCKS,),
        in_specs=[
            pl.BlockSpec((tb, D_MODEL), lambda i: (i, 0)),
            _resident((1, D_MODEL)),
            _resident((D_MODEL, 2 * D_FF)),
            _resident((D_MODEL, 2 * D_FF)),
            _resident((D_FF, D_MODEL)),
            _resident((D_FF, D_MODEL)),
        ],
        out_specs=pl.BlockSpec((tb, D_MODEL), lambda i: (i, 0)),
        out_shape=jax.ShapeDtypeStruct((T_PAD, D_MODEL), F32),
        compiler_params=_params(),
        name="dense_ffn",
    )(x, norm_g, w_gu, w_gu_lo, w_down, w_down_lo)


def _mla_proj_kernel(x_ref, g_ref, wdq_ref, qg_ref, wq_ref, wdkv_ref, kvg_ref, wukv_ref,
                     cos_ref, sin_ref, q_ref, k_ref, v_ref, lat_ref, kr_ref):
    hd_w = N_HEADS * QK_NOPE
    x = x_ref[...]
    h = _rms_norm(x, g_ref[...]).astype(BF16)
    cq = _rms_norm(_dot(h, wdq_ref[...]), qg_ref[...]).astype(BF16)
    q = _dot(cq, wq_ref[...])
    cos = cos_ref[...]
    sin = sin_ref[...]
    ckv = _dot(h, wdkv_ref[...])
    c = _rms_norm(ckv[:, :KV_LORA], kvg_ref[...])
    lat_ref[...] = c
    kr = ckv[:, KV_LORA:KV_LORA + 128] * cos + ckv[:, KV_LORA + 128:] * sin
    kr_ref[...] = kr[:, :QK_ROPE]
    krb = kr.astype(BF16)
    kv = _dot(c.astype(BF16), wukv_ref[...])
    for hd in range(N_HEADS):
        lo, hi = hd * 128, (hd + 1) * 128
        qr = q[:, hd_w + lo:hd_w + hi] * cos + q[:, 2 * hd_w + lo:2 * hd_w + hi] * sin
        q_ref[hd, :, 0:128] = q[:, lo:hi].astype(BF16)
        q_ref[hd, :, 128:256] = qr.astype(BF16)
        k_ref[hd, :, 0:128] = kv[:, lo:hi].astype(BF16)
        k_ref[hd, :, 128:256] = krb
        v_ref[hd] = kv[:, hd_w + lo:hd_w + hi].astype(BF16)


def _mla_proj(x, norm_g, w_dq, q_norm_g, w_q, w_dkv, kv_norm_g, w_ukv, cos_t, sin_t):
    tb = TOKEN_BLOCK
    return pl.pallas_call(
        _mla_proj_kernel,
        grid=(N_BLOCKS,),
        in_specs=[
            pl.BlockSpec((tb, D_MODEL), lambda i: (i, 0)),
            _resident((1, D_MODEL)),
            _resident((D_MODEL, Q_LORA)),
            _resident((1, Q_LORA)),
            _resident((Q_LORA, 3 * N_HEADS * 128)),
            _resident((D_MODEL, KV_LORA + 256)),
            _resident((1, KV_LORA)),
            _resident((KV_LORA, 2 * N_HEADS * 128)),
            pl.BlockSpec((tb, 128), lambda i: (i, 0)),
            pl.BlockSpec((tb, 128), lambda i: (i, 0)),
        ],
        out_specs=[
            pl.BlockSpec((N_HEADS, tb, QK_PAD), lambda i: (0, i, 0)),
            pl.BlockSpec((N_HEADS, tb, QK_PAD), lambda i: (0, i, 0)),
            pl.BlockSpec((N_HEADS, tb, V_HEAD), lambda i: (0, i, 0)),
            pl.BlockSpec((tb, KV_LORA), lambda i: (i, 0)),
            pl.BlockSpec((tb, QK_ROPE), lambda i: (i, 0)),
        ],
        out_shape=[
            jax.ShapeDtypeStruct((N_HEADS, T_PAD, QK_PAD), BF16),
            jax.ShapeDtypeStruct((N_HEADS, T_PAD, QK_PAD), BF16),
            jax.ShapeDtypeStruct((N_HEADS, T_PAD, V_HEAD), BF16),
            jax.ShapeDtypeStruct((T_PAD, KV_LORA), F32),
            jax.ShapeDtypeStruct((T_PAD, QK_ROPE), F32),
        ],
        compiler_params=_params(),
        name="mla_proj",
    )(x, norm_g, w_dq, q_norm_g, w_q, w_dkv, kv_norm_g, w_ukv, cos_t, sin_t)


def _flash_kernel(q_ref, k_ref, v_ref, o_ref):
    tq, tk = ATTN_TQ, ATTN_TK
    qi = pl.program_id(2)
    q = q_ref[0]

    def step(j, carry, masked):
        m, l, acc = carry
        start = pl.multiple_of(j * tk, tk)
        k = k_ref[0, pl.ds(start, tk), :]
        v = v_ref[0, pl.ds(start, tk), :]
        s = _dot_nt(q, k)
        if masked:
            row = lax.broadcasted_iota(jnp.int32, (tq, tk), 0)
            col = lax.broadcasted_iota(jnp.int32, (tq, tk), 1)
            s = jnp.where(col <= row, s, NEG_INF)
        m_new = jnp.maximum(m, jnp.max(s, axis=-1, keepdims=True))
        alpha = jnp.exp((m - m_new) * ATTN_SCALE)
        p = jnp.exp((s - m_new) * ATTN_SCALE)
        l = alpha * l + jnp.sum(p, axis=-1, keepdims=True)
        acc = alpha * acc + _dot(p.astype(BF16), v)
        return m_new, l, acc

    init = (jnp.full((tq, 1), NEG_INF, F32), jnp.zeros((tq, 1), F32), jnp.zeros((tq, V_HEAD), F32))
    carry = lax.fori_loop(0, qi, functools.partial(step, masked=False), init)
    _, l, acc = step(qi, carry, masked=True)
    o_ref[...] = (acc * (1.0 / l)).astype(BF16)


def _flash_attention(q, k, v):
    assert ATTN_TQ == ATTN_TK
    nq = SEQ // ATTN_TQ
    return pl.pallas_call(
        _flash_kernel,
        grid=(BATCH, N_HEADS, nq),
        in_specs=[
            pl.BlockSpec((1, ATTN_TQ, QK_PAD), lambda b, h, i: (h, b * nq + i, 0)),
            pl.BlockSpec((1, SEQ, QK_PAD), lambda b, h, i: (h, b, 0)),
            pl.BlockSpec((1, SEQ, V_HEAD), lambda b, h, i: (h, b, 0)),
        ],
        out_specs=pl.BlockSpec((ATTN_TQ, V_HEAD), lambda b, h, i: (b * nq + i, h)),
        out_shape=jax.ShapeDtypeStruct((N_PROMPT, N_HEADS * V_HEAD), BF16),
        compiler_params=_params(),
        name="flash_attention",
    )(q, k, v)


def _sample_qabs_kernel(q_ref, wukt_ref, qa_ref):
    for hd in range(N_HEADS):
        q = q_ref[hd]
        qa_ref[hd, :, 0:KV_LORA] = _dot(q[:, :QK_NOPE], wukt_ref[hd])
        qa_ref[hd, :, KV_LORA:] = q[:, QK_NOPE:].astype(F32)


def _sample_qabs(q, w_ukt):
    blk = N_PROMPT // N_SAMPLE
    return pl.pallas_call(
        _sample_qabs_kernel,
        grid=(1,),
        in_specs=[
            pl.BlockSpec((N_HEADS, N_SAMPLE, QK_PAD), lambda i: (0, blk, 0)),
            pl.BlockSpec((N_HEADS, QK_NOPE, KV_LORA), lambda i: (0, 0, 0)),
        ],
        out_specs=pl.BlockSpec((N_HEADS, N_SAMPLE, KV_LORA + 128), lambda i: (0, 0, 0)),
        out_shape=jax.ShapeDtypeStruct((N_HEADS, N_SAMPLE, KV_LORA + 128), F32),
        name="sample_q_absorb",
    )(q, w_ukt)


def _page_copies(pt_ref, lat_hbm, kr_hbm, lat_buf, kr_buf, sem, seq, slot, page):
    pid = pt_ref[seq * N_PAGES + page]
    rows = pl.ds(pl.multiple_of(page * PAGE_SIZE, PAGE_SIZE), PAGE_SIZE)
    return (
        pltpu.make_async_copy(lat_hbm.at[pid], lat_buf.at[slot, rows], sem.at[0, slot]),
        pltpu.make_async_copy(kr_hbm.at[pid], kr_buf.at[slot, rows], sem.at[1, slot]),
    )


def _sample_attn_kernel(pt_ref, qa_ref, cn_ref, krn_ref, lat_hbm, kr_hbm, o_ref,
                        lat_buf, kr_buf, sem):
    n = pl.program_id(0)
    slot = lax.rem(n, 2)
    copies = functools.partial(_page_copies, pt_ref, lat_hbm, kr_hbm, lat_buf, kr_buf, sem)

    def start_pages(seq, slot_):
        def body(page, _):
            for cp in copies(seq, slot_, page):
                cp.start()
            return 0
        lax.fori_loop(0, N_PAGES, body, 0)

    @pl.when(n == 0)
    def _():
        start_pages(0, 0)

    @pl.when(n + 1 < pl.num_programs(0))
    def _():
        start_pages(n + 1, 1 - slot)

    def wait_body(page, _):
        for cp in copies(n, slot, page):
            cp.wait()
        return 0
    lax.fori_loop(0, N_PAGES, wait_body, 0)

    qa = qa_ref[0]
    ql = qa[:, :KV_LORA].astype(BF16)
    qr = qa[:, KV_LORA:KV_LORA + QK_ROPE].astype(BF16)
    lat = lat_buf[slot].astype(BF16)
    kr = kr_buf[slot].astype(BF16)
    s = (_dot_nt(ql, lat) + _dot_nt(qr, kr)) * ATTN_SCALE
    cn = cn_ref[pl.ds(n, 1), :]
    krn = krn_ref[pl.ds(n, 1), :]
    s_new = (jnp.sum(ql.astype(F32) * cn.astype(BF16).astype(F32), axis=-1, keepdims=True)
             + jnp.sum(qr.astype(F32) * krn.astype(BF16).astype(F32), axis=-1, keepdims=True)) * ATTN_SCALE
    m = jnp.maximum(jnp.max(s, axis=-1, keepdims=True), s_new)
    p = jnp.exp(s - m)
    p_new = jnp.exp(s_new - m)
    inv_l = 1.0 / (jnp.sum(p, axis=-1, keepdims=True) + p_new)
    o_ref[0] = _dot((p * inv_l).astype(BF16), lat) + (p_new * inv_l) * cn


def _sample_attention(page_table, qa_t, lat, kr, lat_pool, kr_pool):
    blk = N_PROMPT // N_SAMPLE
    grid_spec = pltpu.PrefetchScalarGridSpec(
        num_scalar_prefetch=1,
        grid=(N_SAMPLE,),
        in_specs=[
            pl.BlockSpec((1, N_HEADS, KV_LORA + 128), lambda n, pt: (n, 0, 0)),
            pl.BlockSpec((N_SAMPLE, KV_LORA), lambda n, pt: (blk, 0)),
            pl.BlockSpec((N_SAMPLE, QK_ROPE), lambda n, pt: (blk, 0)),
            pl.BlockSpec(memory_space=pl.ANY),
            pl.BlockSpec(memory_space=pl.ANY),
        ],
        out_specs=pl.BlockSpec((1, N_HEADS, KV_LORA), lambda n, pt: (n, 0, 0)),
        scratch_shapes=[
            pltpu.VMEM((2, PAST_LEN, KV_LORA), F32),
            pltpu.VMEM((2, PAST_LEN, QK_ROPE), F32),
            pltpu.SemaphoreType.DMA((2, 2)),
        ],
    )
    return pl.pallas_call(
        _sample_attn_kernel,
        grid_spec=grid_spec,
        out_shape=jax.ShapeDtypeStruct((N_SAMPLE, N_HEADS, KV_LORA), F32),
        compiler_params=_params(),
        name="sample_attention",
    )(page_table, qa_t, lat, kr, lat_pool, kr_pool)


def _sample_oup_kernel(ol_ref, wuv_ref, o_ref):
    o_ref[...] = jnp.zeros(o_ref.shape, BF16)
    for hd in range(N_HEADS):
        o = _dot(ol_ref[hd].astype(BF16), wuv_ref[hd])
        o_ref[0:N_SAMPLE, hd * V_HEAD:(hd + 1) * V_HEAD] = o.astype(BF16)


def _sample_oup(ol_t, w_uvh):
    return pl.pallas_call(
        _sample_oup_kernel,
        grid=(1,),
        in_specs=[
            pl.BlockSpec((N_HEADS, N_SAMPLE, KV_LORA), lambda i: (0, 0, 0)),
            pl.BlockSpec((N_HEADS, KV_LORA, V_HEAD), lambda i: (0, 0, 0)),
        ],
        out_specs=pl.BlockSpec((TOKEN_BLOCK, N_HEADS * V_HEAD), lambda i: (0, 0)),
        out_shape=jax.ShapeDtypeStruct((TOKEN_BLOCK, N_HEADS * V_HEAD), BF16),
        name="sample_o_up",
    )(ol_t, w_uvh)


def _attn_out_kernel(x_ref, op_ref, os_ref, wo_ref, g_ref, wr_ref, wrlo_ref, x1_ref, h2_ref, route_ref):
    tb = TOKEN_BLOCK
    is_sample = pl.program_id(0) == SAMPLE_BLOCK
    o = jnp.where(is_sample, os_ref[...], op_ref[...])
    x1 = x_ref[...] + _dot(o, wo_ref[...])
    x1_ref[...] = x1
    h2 = _rms_norm(x1, g_ref[...])
    h2_ref[...] = h2
    logits = _dot_split(h2, wr_ref, wrlo_ref)
    lane = lax.broadcasted_iota(jnp.int32, (tb, 128), 1)
    lane_f = lane.astype(F32)
    lg = jnp.where(lane < N_EXPERTS, logits, -jnp.inf)
    v1 = jnp.max(lg, axis=-1, keepdims=True)
    i1 = jnp.min(jnp.where(lg == v1, lane_f, 128.0), axis=-1, keepdims=True)
    lg2 = jnp.where(lane_f == i1, -jnp.inf, lg)
    v2 = jnp.max(lg2, axis=-1, keepdims=True)
    i2 = jnp.min(jnp.where(lg2 == v2, lane_f, 128.0), axis=-1, keepdims=True)
    e2 = jnp.exp(v2 - v1)
    den = 1.0 + e2
    g1 = 1.0 / den
    g2 = e2 / den
    route_ref[...] = jnp.where(lane == 0, i1, jnp.where(lane == 1, i2,
                               jnp.where(lane == 2, g1, jnp.where(lane == 3, g2, 0.0))))


def _attn_out(x, o_prompt, o_sample, w_o, norm_g, w_router):
    tb = TOKEN_BLOCK
    return pl.pallas_call(
        _attn_out_kernel,
        grid=(N_BLOCKS,),
        in_specs=[
            pl.BlockSpec((tb, D_MODEL), lambda i: (i, 0)),
            pl.BlockSpec((tb, D_MODEL), lambda i: (jnp.minimum(i, SAMPLE_BLOCK - 1), 0)),
            pl.BlockSpec((tb, D_MODEL), lambda i: (0, 0)),
            _resident((N_HEADS * V_HEAD, D_MODEL)),
            _resident((1, D_MODEL)),
            _resident((D_MODEL, 128)),
            _resident((D_MODEL, 128)),
        ],
        out_specs=[
            pl.BlockSpec((tb, D_MODEL), lambda i: (i, 0)),
            pl.BlockSpec((tb, D_MODEL), lambda i: (i, 0)),
            pl.BlockSpec((tb, 128), lambda i: (i, 0)),
        ],
        out_shape=[
            jax.ShapeDtypeStruct((T_PAD, D_MODEL), F32),
            jax.ShapeDtypeStruct((T_PAD, D_MODEL), F32),
            jax.ShapeDtypeStruct((T_PAD, 128), F32),
        ],
        compiler_params=_params(),
        name="attn_out_router",
    )(x, o_prompt, o_sample, w_o, norm_g, *_split_bf16(w_router))


def _moe_kernel(texp_ref, nused_ref, src_ref, dst_ref,
                h_hbm, gate_ref, wg_ref, wu_ref, wd_ref, y_hbm,
                xg, obuf, gsem, ssem, zsem):
    tm = MOE_TM
    i = pl.program_id(0)
    n_used = nused_ref[0]
    slot = lax.rem(i, 2)

    def start_gather(tile, slot_):
        def body(r, _):
            idx = src_ref[tile * tm + r]
            pltpu.make_async_copy(h_hbm.at[pl.ds(idx, 1)], xg.at[slot_, pl.ds(r, 1)],
                                  gsem.at[slot_]).start()
            return 0
        lax.fori_loop(0, tm, body, 0, unroll=8)

    def scatter_all():
        return pltpu.make_async_copy(obuf, y_hbm.at[pl.ds(0, tm)], ssem)

    @pl.when(i == 0)
    def _():
        obuf[...] = jnp.zeros(obuf.shape, F32)
        n_pad = T_PAD - N_TOK
        for first, count in ((N_TOK, n_pad), (T_PAD + N_TOK, n_pad), (2 * T_PAD, MOE_TRASH_ROWS)):
            for off in range(0, count, tm):
                n = min(tm, count - off)
                cp = pltpu.make_async_copy(obuf.at[pl.ds(0, n)], y_hbm.at[pl.ds(first + off, n)], zsem)
                cp.start()
                cp.wait()
        start_gather(0, 0)

    @pl.when(i < n_used)
    def _():
        @pl.when(i + 1 < n_used)
        def _():
            start_gather(i + 1, 1 - slot)

        pltpu.make_async_copy(h_hbm.at[pl.ds(0, tm)], xg.at[slot], gsem.at[slot]).wait()
        x = xg[slot].astype(BF16)
        acc = jnp.zeros((tm, D_MODEL), F32)
        for c in range(D_FF_EXPERT // MOE_FC):
            cols = slice(c * MOE_FC, (c + 1) * MOE_FC)
            a = (_silu(_dot(x, wg_ref[0, :, cols])) * _dot(x, wu_ref[0, :, cols])).astype(BF16)
            acc = acc + _dot(a, wd_ref[0, cols, :])

        @pl.when(i > 0)
        def _():
            scatter_all().wait()

        obuf[...] = acc * gate_ref[...]

        def body(r, _):
            d = dst_ref[i * tm + r]
            pltpu.make_async_copy(obuf.at[pl.ds(r, 1)], y_hbm.at[pl.ds(d, 1)], ssem).start()
            return 0
        lax.fori_loop(0, tm, body, 0, unroll=8)

        @pl.when(i == n_used - 1)
        def _():
            scatter_all().wait()


def _moe_layer(h2, tile_expert, n_used, src, dst, gate_sorted, w_gate, w_up, w_down):
    tm = MOE_TM
    grid_spec = pltpu.PrefetchScalarGridSpec(
        num_scalar_prefetch=4,
        grid=(MOE_TILES,),
        in_specs=[
            pl.BlockSpec(memory_space=pl.ANY),
            pl.BlockSpec((tm, 1), lambda i, te, nu, s, d: (i, 0)),
            pl.BlockSpec((1, D_MODEL, D_FF_EXPERT), lambda i, te, nu, s, d: (te[i], 0, 0)),
            pl.BlockSpec((1, D_MODEL, D_FF_EXPERT), lambda i, te, nu, s, d: (te[i], 0, 0)),
            pl.BlockSpec((1, D_FF_EXPERT, D_MODEL), lambda i, te, nu, s, d: (te[i], 0, 0)),
        ],
        out_specs=pl.BlockSpec(memory_space=pl.ANY),
        scratch_shapes=[
            pltpu.VMEM((2, tm, D_MODEL), F32),
            pltpu.VMEM((tm, D_MODEL), F32),
            pltpu.SemaphoreType.DMA((2,)),
            pltpu.SemaphoreType.DMA(()),
            pltpu.SemaphoreType.DMA(()),
        ],
    )
    return pl.pallas_call(
        _moe_kernel,
        grid_spec=grid_spec,
        out_shape=jax.ShapeDtypeStruct((Y_ROWS, D_MODEL), F32),
        compiler_params=_params(),
        name="moe_experts",
    )(tile_expert, n_used, src, dst, h2, gate_sorted, w_gate, w_up, w_down)


def _moe_plan(route):
    tm = MOE_TM
    r = route[:N_TOK]
    expert = jnp.concatenate([r[:, 0], r[:, 1]]).astype(jnp.int32)
    gate = jnp.concatenate([r[:, 2], r[:, 3]])
    tok = jnp.arange(N_TOK, dtype=jnp.int32)
    flat = jnp.concatenate([tok, T_PAD + tok])
    onehot = (expert[:, None] == jnp.arange(N_EXPERTS, dtype=jnp.int32)[None, :]).astype(jnp.int32)
    csum = jnp.cumsum(onehot, axis=0)
    counts = csum[-1]
    rank = jnp.sum(csum * onehot, axis=1) - 1
    tiles = (counts + tm - 1) // tm
    tile_end = jnp.cumsum(tiles)
    tile_start = tile_end - tiles
    pos = tile_start[expert] * tm + rank
    n_used = tile_end[-1]
    dst = jnp.full((MOE_ROWS,), -1, jnp.int32).at[pos].set(flat)
    gate_sorted = jnp.zeros((MOE_ROWS,), F32).at[pos].set(gate)
    is_pad = dst < 0
    src = jnp.where(is_pad, 0, dst % T_PAD)
    pad_rank = jnp.cumsum(is_pad.astype(jnp.int32)) - 1
    dst = jnp.where(is_pad, 2 * T_PAD + jnp.minimum(pad_rank, MOE_TRASH_ROWS - 1), dst)
    tile_id = jnp.arange(MOE_TILES, dtype=jnp.int32)
    texp = jnp.sum((tile_id[:, None] >= tile_end[None, :]).astype(jnp.int32), axis=1)
    last = jnp.sum((n_used - 1 >= tile_end).astype(jnp.int32))
    texp = jnp.minimum(jnp.where(tile_id < n_used, texp, last), N_EXPERTS - 1)
    return texp, n_used.reshape(1), src, dst, gate_sorted.reshape(MOE_ROWS, 1)


def _combine_kernel(x_ref, y0_ref, y1_ref, o_ref):
    o_ref[...] = x_ref[...] + y0_ref[...] + y1_ref[...]


def _combine_norm_kernel(x_ref, y0_ref, y1_ref, g_ref, o_ref):
    o_ref[...] = _rms_norm(x_ref[...] + y0_ref[...] + y1_ref[...], g_ref[...])


def _combine(x1, y, final_g=None):
    tb = TOKEN_BLOCK
    in_specs = [
        pl.BlockSpec((tb, D_MODEL), lambda i: (i, 0)),
        pl.BlockSpec((tb, D_MODEL), lambda i: (i, 0)),
        pl.BlockSpec((tb, D_MODEL), lambda i: (N_BLOCKS + i, 0)),
    ]
    args = [x1, y, y]
    if final_g is not None:
        in_specs.append(_resident((1, D_MODEL)))
        args.append(final_g)
    return pl.pallas_call(
        _combine_kernel if final_g is None else _combine_norm_kernel,
        grid=(N_BLOCKS,),
        in_specs=in_specs,
        out_specs=pl.BlockSpec((tb, D_MODEL), lambda i: (i, 0)),
        out_shape=jax.ShapeDtypeStruct((T_PAD, D_MODEL), F32),
        name="moe_combine",
    )(*args)


def _swap_halves(w):
    half = w.shape[-1] // 2
    return jnp.concatenate([w[..., half:], w[..., :half]], axis=-1)


def _pad_last(w, n):
    return jnp.pad(w, [(0, 0)] * (w.ndim - 1) + [(0, n - w.shape[-1])])


def _rope_tables():
    half = QK_ROPE // 2
    pos = jnp.concatenate([
        jnp.tile(jnp.arange(SEQ, dtype=jnp.int32), BATCH),
        jnp.full((N_SAMPLE,), PAST_LEN, jnp.int32),
        jnp.zeros((T_PAD - N_TOK,), jnp.int32),
    ])
    inv_freq = ROPE_THETA ** (-jnp.arange(half, dtype=F32) / half)
    ang = pos.astype(F32)[:, None] * inv_freq[None, :]
    cos, sin = jnp.cos(ang), jnp.sin(ang)
    return (jnp.concatenate([cos, cos, cos, cos], axis=-1),
            jnp.concatenate([-sin, sin, -sin, sin], axis=-1))


def _sgu_spatial_params(w_s, b_s):
    causal = jnp.tril(jnp.ones((CHUNK, CHUNK), dtype=bool))
    ws_causal = jnp.where(causal[None], w_s, 0.0).astype(BF16)
    bias_full = jnp.repeat(b_s.T, CHUNK, axis=1)
    ws0_row = jnp.repeat(w_s[:, 0, 0], CHUNK)[None, :]
    bias0_row = jnp.repeat(b_s[:, 0], CHUNK)[None, :]
    return ws_causal, bias_full, ws0_row, bias0_row


def _mla_params(w_dq, w_uq, w_dkv, w_uk, w_uv):
    qn = w_uq[:, :, :QK_NOPE].reshape(Q_LORA, N_HEADS * QK_NOPE)
    qr = w_uq[:, :, QK_NOPE:]
    qr_pad = _pad_last(qr, 128).reshape(Q_LORA, N_HEADS * 128)
    qrs_pad = _pad_last(_swap_halves(qr), 128).reshape(Q_LORA, N_HEADS * 128)
    w_q = jnp.concatenate([qn, qr_pad, qrs_pad], axis=1).astype(BF16)
    kr = w_dkv[:, KV_LORA:]
    w_dkv3 = jnp.concatenate(
        [w_dkv[:, :KV_LORA], _pad_last(kr, 128), _pad_last(_swap_halves(kr), 128)], axis=1).astype(BF16)
    w_ukv = jnp.concatenate([w_uk.reshape(KV_LORA, -1), w_uv.reshape(KV_LORA, -1)], axis=1).astype(BF16)
    w_ukt = w_uk.transpose(1, 2, 0).astype(BF16)
    w_uvh = w_uv.transpose(1, 0, 2).astype(BF16)
    return w_dq.astype(BF16), w_q, w_dkv3, w_ukv, w_ukt, w_uvh


def kernel(x_prompt, x_sample, cache_latent, cache_k_rope, page_table, norm_mix_g, norm_ffn_g, final_norm_g, sgu_w_in, sgu_ln_g, sgu_ln_b, sgu_w_s, sgu_b_s, sgu_w_out, mla_w_dq, mla_q_norm_g, mla_w_uq, mla_w_dkv, mla_kv_norm_g, mla_w_uk, mla_w_uv, mla_w_o, ffn_w_gate, ffn_w_up, ffn_w_down, moe_w_router, moe_w_gate, moe_w_up, moe_w_down):
    x = jnp.concatenate([
        x_prompt.reshape(N_PROMPT, D_MODEL),
        x_sample.reshape(N_SAMPLE, D_MODEL),
        jnp.zeros((T_PAD - N_TOK, D_MODEL), F32),
    ])
    cos_t, sin_t = _rope_tables()
    pt_flat = page_table.reshape(-1)
    lat_all, kr_all, v_s = [], [], []
    y = None
    for i in range(DEPTH):
        j = i // 2
        g_mix = norm_mix_g[i].reshape(1, D_MODEL)
        g_ffn = norm_ffn_g[i].reshape(1, D_MODEL)
        if i % 2 == 0:
            x, vs = _sgu_layer(x, g_mix, sgu_w_in[j], sgu_ln_g[j].reshape(1, -1),
                               sgu_ln_b[j].reshape(1, -1), *_sgu_spatial_params(sgu_w_s[j], sgu_b_s[j]),
                               sgu_w_out[j])
            v_s.append(vs.reshape(N_SAMPLE, 1, D_SGU))
            w_gu = jnp.concatenate([ffn_w_gate[j], ffn_w_up[j]], axis=1)
            x = _ffn_layer(x, g_ffn, w_gu, ffn_w_down[j])
        else:
            w_dq, w_q, w_dkv3, w_ukv, w_ukt, w_uvh = _mla_params(
                mla_w_dq[j], mla_w_uq[j], mla_w_dkv[j], mla_w_uk[j], mla_w_uv[j])
            q, k, v, lat, kr = _mla_proj(x, g_mix, w_dq, mla_q_norm_g[j].reshape(1, -1), w_q, w_dkv3,
                                         mla_kv_norm_g[j].reshape(1, -1), w_ukv, cos_t, sin_t)
            lat_all.append(lat)
            kr_all.append(kr)
            o_prompt = _flash_attention(q, k, v)
            qa = _sample_qabs(q, w_ukt)
            o_lat = _sample_attention(pt_flat, qa.transpose(1, 0, 2), lat, kr,
                                      cache_latent[j], cache_k_rope[j])
            o_sample = _sample_oup(o_lat.transpose(1, 0, 2), w_uvh)
            w_r = _pad_last(moe_w_router[j], 128)
            x1, h2, route = _attn_out(x, o_prompt, o_sample, mla_w_o[j].astype(BF16), g_ffn, w_r)
            texp, n_used, src, dst, gate_sorted = _moe_plan(route)
            y = _moe_layer(h2, texp, n_used, src, dst, gate_sorted, moe_w_gate[j].astype(BF16),
                           moe_w_up[j].astype(BF16), moe_w_down[j].astype(BF16))
            final_g = final_norm_g.reshape(1, D_MODEL) if i == DEPTH - 1 else None
            x = _combine(x1, y, final_g)

    def split(a, width):
        return (a[:N_PROMPT].reshape(BATCH, SEQ, width), a[N_PROMPT:N_TOK].reshape(N_SAMPLE, 1, width))

    y_prompt, y_sample = split(x, D_MODEL)
    lat_p, lat_s = zip(*[split(a, KV_LORA) for a in lat_all])
    kr_p, kr_s = zip(*[split(a, QK_ROPE) for a in kr_all])
    return (y_prompt, y_sample, jnp.stack(lat_p), jnp.stack(kr_p), jnp.stack(lat_s), jnp.stack(kr_s),
            jnp.stack(v_s))
```

```python
import functools
import math

import jax
import jax.numpy as jnp
from jax import lax
from jax.experimental import pallas as pl
from jax.experimental.pallas import tpu as pltpu

F32 = jnp.float32
BF16 = jnp.bfloat16

D_MODEL = 1024
BATCH = 2
SEQ = 8192
DEPTH = 4
N_SAMPLE = 128
PAST_LEN = 8192
PAGE_SIZE = 128
N_PAGES = PAST_LEN // PAGE_SIZE

CHUNK = 128
D_SGU = 2 * D_MODEL
N_SGU_GROUPS = D_SGU // CHUNK

N_HEADS = 8
QK_NOPE = 128
QK_ROPE = 64
V_HEAD = 128
Q_LORA = 384
KV_LORA = 256
ROPE_THETA = 10000.0
ATTN_SCALE = 1.0 / math.sqrt(QK_NOPE + QK_ROPE)
QK_PAD = 256

D_FF = 5 * D_MODEL // 2
N_EXPERTS = 8
D_FF_EXPERT = 7 * D_MODEL // 2
NORM_EPS = 1e-6
LN_EPS = 1e-5
NEG_INF = -1e30

TOKEN_BLOCK = 512
N_PROMPT = BATCH * SEQ
N_TOK = N_PROMPT + N_SAMPLE
N_BLOCKS = pl.cdiv(N_TOK, TOKEN_BLOCK)
T_PAD = N_BLOCKS * TOKEN_BLOCK
SAMPLE_BLOCK = N_PROMPT // TOKEN_BLOCK
CHUNKS_PER_BLOCK = TOKEN_BLOCK // CHUNK

ATTN_TQ = 512
ATTN_TK = 512
ATTN_HEADS = 2

MOE_TM = 256
MOE_TILES = 2 * N_TOK // MOE_TM + N_EXPERTS
MOE_ROWS = MOE_TILES * MOE_TM
MOE_FC = 512
MOE_PAD_ROWS = MOE_ROWS - 2 * N_TOK
MOE_TRASH_ROWS = MOE_PAD_ROWS + MOE_TM
Y_ROWS = 2 * T_PAD + MOE_TRASH_ROWS

VMEM_LIMIT = 56 * 1024 * 1024


def _params(**kw):
    return pltpu.CompilerParams(vmem_limit_bytes=VMEM_LIMIT, **kw)


def _resident(shape):
    return pl.BlockSpec(shape, lambda *_: (0,) * len(shape), pipeline_mode=pl.Buffered(1))


def _rms_norm(x, g):
    return x * lax.rsqrt(jnp.mean(x * x, axis=-1, keepdims=True) + NORM_EPS) * g


def _gelu(x):
    return 0.5 * x * (1.0 + lax.erf(x * math.sqrt(0.5)))


def _silu(x):
    return x * (1.0 / (1.0 + jnp.exp(-x)))


def _dot(a, b):
    return jnp.dot(a, b, preferred_element_type=F32)


def _split_bf16(w):
    hi = w.astype(BF16)
    return hi, (w - hi.astype(F32)).astype(BF16)


def _split_weights_kernel(w_ref, hi_ref, lo_ref):
    hi_ref[...], lo_ref[...] = _split_bf16(w_ref[...])


def _split_weights(w):
    rows, cols = w.shape
    br = 256
    spec = pl.BlockSpec((br, cols), lambda i: (i, 0))
    return pl.pallas_call(
        _split_weights_kernel,
        grid=(rows // br,),
        in_specs=[spec],
        out_specs=[spec, spec],
        out_shape=[jax.ShapeDtypeStruct(w.shape, BF16)] * 2,
        name="split_weights",
    )(w)


def _dot_split(a, w_hi_ref, w_lo_ref):
    a_hi, a_lo = _split_bf16(a)
    w_hi = w_hi_ref[...]
    return _dot(a_hi, w_hi) + (_dot(a_lo, w_hi) + _dot(a_hi, w_lo_ref[...]))


def _dot_nt(a, b):
    return lax.dot_general(a, b, (((1,), (1,)), ((), ())), preferred_element_type=F32)


def _sgu_gelu_ln(z, lng_ref, lnb_ref):
    z = _gelu(z)
    u = z[:, :D_SGU]
    v = z[:, D_SGU:]
    vc = v - jnp.mean(v, axis=-1, keepdims=True)
    vn = vc * lax.rsqrt(jnp.mean(vc * vc, axis=-1, keepdims=True) + LN_EPS)
    return u, vn * lng_ref[...] + lnb_ref[...]


def _sgu_kernel(x_ref, g_ref, win_ref, winlo_ref, lng_ref, lnb_ref, ws_ref, bias_ref, ws0_ref, bias0_ref,
                wout_ref, woutlo_ref, o_ref, v_ref, gated_ref):
    is_sample = pl.program_id(0) == SAMPLE_BLOCK

    @pl.when(is_sample)
    def _():
        x = x_ref[0:N_SAMPLE, :]
        h = _rms_norm(x, g_ref[...])
        u, vn = _sgu_gelu_ln(_dot_split(h, win_ref, winlo_ref), lng_ref, lnb_ref)
        v_ref[...] = vn
        gated = u * (vn * ws0_ref[...] + bias0_ref[...])
        o_ref[0:N_SAMPLE, :] = x + _dot_split(gated, wout_ref, woutlo_ref)
        o_ref[N_SAMPLE:, :] = jnp.zeros((TOKEN_BLOCK - N_SAMPLE, D_MODEL), F32)

    @pl.when(jnp.logical_not(is_sample))
    def _():
        x = x_ref[...]
        h = _rms_norm(x, g_ref[...]).astype(BF16)
        u, vn = _sgu_gelu_ln(_dot(h, win_ref[...]), lng_ref, lnb_ref)
        vb = vn.astype(BF16)
        for g in range(N_SGU_GROUPS):
            cols = slice(g * CHUNK, (g + 1) * CHUNK)
            rhs = jnp.concatenate(
                [vb[c * CHUNK:(c + 1) * CHUNK, cols] for c in range(CHUNKS_PER_BLOCK)], axis=1)
            mixed = _dot(ws_ref[g], rhs)
            for c in range(CHUNKS_PER_BLOCK):
                rows = slice(c * CHUNK, (c + 1) * CHUNK)
                m = mixed[:, c * CHUNK:(c + 1) * CHUNK] + bias_ref[:, cols]
                gated_ref[rows, cols] = (u[rows, cols] * m).astype(BF16)
        o_ref[...] = x + _dot(gated_ref[...], wout_ref[...])


def _sgu_layer(x, norm_g, w_in, ln_g, ln_b, ws_causal, bias_full, ws0_row, bias0_row, w_out):
    w_in, w_in_lo = _split_weights(w_in)
    w_out, w_out_lo = _split_weights(w_out)
    tb = TOKEN_BLOCK
    return pl.pallas_call(
        _sgu_kernel,
        grid=(N_BLOCKS,),
        in_specs=[
            pl.BlockSpec((tb, D_MODEL), lambda i: (i, 0)),
            _resident((1, D_MODEL)),
            _resident((D_MODEL, 2 * D_SGU)),
            _resident((D_MODEL, 2 * D_SGU)),
            _resident((1, D_SGU)),
            _resident((1, D_SGU)),
            _resident((N_SGU_GROUPS, CHUNK, CHUNK)),
            _resident((CHUNK, D_SGU)),
            _resident((1, D_SGU)),
            _resident((1, D_SGU)),
            _resident((D_SGU, D_MODEL)),
            _resident((D_SGU, D_MODEL)),
        ],
        out_specs=[
            pl.BlockSpec((tb, D_MODEL), lambda i: (i, 0)),
            pl.BlockSpec((N_SAMPLE, D_SGU), lambda i: (0, 0)),
        ],
        out_shape=[
            jax.ShapeDtypeStruct((T_PAD, D_MODEL), F32),
            jax.ShapeDtypeStruct((N_SAMPLE, D_SGU), F32),
        ],
        scratch_shapes=[pltpu.VMEM((tb, D_SGU), BF16)],
        compiler_params=_params(),
        name="sgu_mixer",
    )(x, norm_g, w_in, w_in_lo, ln_g, ln_b, ws_causal, bias_full, ws0_row, bias0_row, w_out, w_out_lo)


def _ffn_kernel(x_ref, g_ref, wgu_ref, wgulo_ref, wd_ref, wdlo_ref, o_ref):
    is_sample = pl.program_id(0) == SAMPLE_BLOCK

    @pl.when(is_sample)
    def _():
        x = x_ref[0:N_SAMPLE, :]
        gu = _dot_split(_rms_norm(x, g_ref[...]), wgu_ref, wgulo_ref)
        a = _silu(gu[:, :D_FF]) * gu[:, D_FF:]
        o_ref[0:N_SAMPLE, :] = x + _dot_split(a, wd_ref, wdlo_ref)
        o_ref[N_SAMPLE:, :] = jnp.zeros((TOKEN_BLOCK - N_SAMPLE, D_MODEL), F32)

    @pl.when(jnp.logical_not(is_sample))
    def _():
        x = x_ref[...]
        h = _rms_norm(x, g_ref[...]).astype(BF16)
        gu = _dot(h, wgu_ref[...])
        a = (_silu(gu[:, :D_FF]) * gu[:, D_FF:]).astype(BF16)
        o_ref[...] = x + _dot(a, wd_ref[...])


def _ffn_layer(x, norm_g, w_gu, w_down):
    tb = TOKEN_BLOCK
    w_gu, w_gu_lo = _split_weights(w_gu)
    w_down, w_down_lo = _split_weights(w_down)
    return pl.pallas_call(
        _ffn_kernel,
        grid=(N_BLOCKS,),
        in_specs=[
            pl.BlockSpec((tb, D_MODEL), lambda i: (i, 0)),
            _resident((1, D_MODEL)),
            _resident((D_MODEL, 2 * D_FF)),
            _resident((D_MODEL, 2 * D_FF)),
            _resident((D_FF, D_MODEL)),
            _resident((D_FF, D_MODEL)),
        ],
        out_specs=pl.BlockSpec((tb, D_MODEL), lambda i: (i, 0)),
        out_shape=jax.ShapeDtypeStruct((T_PAD, D_MODEL), F32),
        compiler_params=_params(),
        name="dense_ffn",
    )(x, norm_g, w_gu, w_gu_lo, w_down, w_down_lo)


def _mla_proj_kernel(x_ref, g_ref, wdq_ref, qg_ref, wq_ref, wdkv_ref, kvg_ref, wukv_ref,
                     cos_ref, sin_ref, q_ref, k_ref, v_ref, lat_ref, kr_ref):
    hd_w = N_HEADS * QK_NOPE
    x = x_ref[...]
    h = _rms_norm(x, g_ref[...]).astype(BF16)
    cq = _rms_norm(_dot(h, wdq_ref[...]), qg_ref[...]).astype(BF16)
    q = _dot(cq, wq_ref[...])
    cos = cos_ref[...]
    sin = sin_ref[...]
    ckv = _dot(h, wdkv_ref[...])
    c = _rms_norm(ckv[:, :KV_LORA], kvg_ref[...])
    lat_ref[...] = c
    kr = ckv[:, KV_LORA:KV_LORA + 128] * cos + ckv[:, KV_LORA + 128:] * sin
    kr_ref[...] = kr[:, :QK_ROPE]
    krb = kr.astype(BF16)
    kv = _dot(c.astype(BF16), wukv_ref[...])
    for hd in range(N_HEADS):
        lo, hi = hd * 128, (hd + 1) * 128
        qr = q[:, hd_w + lo:hd_w + hi] * cos + q[:, 2 * hd_w + lo:2 * hd_w + hi] * sin
        q_ref[hd, :, 0:128] = q[:, lo:hi].astype(BF16)
        q_ref[hd, :, 128:256] = qr.astype(BF16)
        k_ref[hd, :, 0:128] = kv[:, lo:hi].astype(BF16)
        k_ref[hd, :, 128:256] = krb
        v_ref[hd] = kv[:, hd_w + lo:hd_w + hi].astype(BF16)


def _mla_proj(x, norm_g, w_dq, q_norm_g, w_q, w_dkv, kv_norm_g, w_ukv, cos_t, sin_t):
    tb = TOKEN_BLOCK
    return pl.pallas_call(
        _mla_proj_kernel,
        grid=(N_BLOCKS,),
        in_specs=[
            pl.BlockSpec((tb, D_MODEL), lambda i: (i, 0)),
            _resident((1, D_MODEL)),
            _resident((D_MODEL, Q_LORA)),
            _resident((1, Q_LORA)),
            _resident((Q_LORA, 3 * N_HEADS * 128)),
            _resident((D_MODEL, KV_LORA + 256)),
            _resident((1, KV_LORA)),
            _resident((KV_LORA, 2 * N_HEADS * 128)),
            pl.BlockSpec((tb, 128), lambda i: (i, 0)),
            pl.BlockSpec((tb, 128), lambda i: (i, 0)),
        ],
        out_specs=[
            pl.BlockSpec((N_HEADS, tb, QK_PAD), lambda i: (0, i, 0)),
            pl.BlockSpec((N_HEADS, tb, QK_PAD), lambda i: (0, i, 0)),
            pl.BlockSpec((N_HEADS, tb, V_HEAD), lambda i: (0, i, 0)),
            pl.BlockSpec((tb, KV_LORA), lambda i: (i, 0)),
            pl.BlockSpec((tb, QK_ROPE), lambda i: (i, 0)),
        ],
        out_shape=[
            jax.ShapeDtypeStruct((N_HEADS, T_PAD, QK_PAD), BF16),
            jax.ShapeDtypeStruct((N_HEADS, T_PAD, QK_PAD), BF16),
            jax.ShapeDtypeStruct((N_HEADS, T_PAD, V_HEAD), BF16),
            jax.ShapeDtypeStruct((T_PAD, KV_LORA), F32),
            jax.ShapeDtypeStruct((T_PAD, QK_ROPE), F32),
        ],
        compiler_params=_params(),
        name="mla_proj",
    )(x, norm_g, w_dq, q_norm_g, w_q, w_dkv, kv_norm_g, w_ukv, cos_t, sin_t)


def _flash_kernel(q_ref, k_ref, v_ref, o_ref):
    tq, tk = ATTN_TQ, ATTN_TK
    qi = pl.program_id(2)
    c = ATTN_SCALE * math.log2(math.e)

    def step(j, carry, masked):
        start = pl.multiple_of(j * tk, tk)
        out = []
        for hd in range(ATTN_HEADS):
            m, l, acc = carry[hd]
            k = k_ref[hd, pl.ds(start, tk), :]
            v = v_ref[hd, pl.ds(start, tk), :]
            s = _dot_nt(q_ref[hd], k)
            if masked:
                row = lax.broadcasted_iota(jnp.int32, (tq, tk), 0)
                col = lax.broadcasted_iota(jnp.int32, (tq, tk), 1)
                s = jnp.where(col <= row, s, NEG_INF)
            m_new = jnp.maximum(m, jnp.max(s, axis=-1, keepdims=True))
            mc = m_new * c
            alpha = jnp.exp2(m * c - mc)
            p = jnp.exp2(s * c - mc)
            l = alpha * l + jnp.sum(p, axis=-1, keepdims=True)
            acc = alpha * acc + _dot(p.astype(BF16), v)
            out.append((m_new, l, acc))
        return tuple(out)

    init = tuple((jnp.full((tq, 1), NEG_INF, F32), jnp.zeros((tq, 1), F32),
                  jnp.zeros((tq, V_HEAD), F32)) for _ in range(ATTN_HEADS))
    carry = lax.fori_loop(0, qi, functools.partial(step, masked=False), init)
    carry = step(qi, carry, masked=True)
    for hd in range(ATTN_HEADS):
        _, l, acc = carry[hd]
        o_ref[:, hd * V_HEAD:(hd + 1) * V_HEAD] = (acc * (1.0 / l)).astype(BF16)


def _flash_attention(q, k, v):
    assert ATTN_TQ == ATTN_TK
    nq = SEQ // ATTN_TQ
    g = ATTN_HEADS
    return pl.pallas_call(
        _flash_kernel,
        grid=(BATCH, N_HEADS // g, nq),
        in_specs=[
            pl.BlockSpec((g, ATTN_TQ, QK_PAD), lambda b, h, i: (h, b * nq + i, 0)),
            pl.BlockSpec((g, SEQ, QK_PAD), lambda b, h, i: (h, b, 0)),
            pl.BlockSpec((g, SEQ, V_HEAD), lambda b, h, i: (h, b, 0)),
        ],
        out_specs=pl.BlockSpec((ATTN_TQ, g * V_HEAD), lambda b, h, i: (b * nq + i, h)),
        out_shape=jax.ShapeDtypeStruct((N_PROMPT, N_HEADS * V_HEAD), BF16),
        compiler_params=_params(),
        name="flash_attention",
    )(q, k, v)


def _sample_qabs_kernel(q_ref, wukt_ref, qa_ref):
    for hd in range(N_HEADS):
        q = q_ref[hd]
        qa_ref[hd, :, 0:KV_LORA] = _dot(q[:, :QK_NOPE], wukt_ref[hd])
        qa_ref[hd, :, KV_LORA:] = q[:, QK_NOPE:].astype(F32)


def _sample_qabs(q, w_ukt):
    blk = N_PROMPT // N_SAMPLE
    return pl.pallas_call(
        _sample_qabs_kernel,
        grid=(1,),
        in_specs=[
            pl.BlockSpec((N_HEADS, N_SAMPLE, QK_PAD), lambda i: (0, blk, 0)),
            pl.BlockSpec((N_HEADS, QK_NOPE, KV_LORA), lambda i: (0, 0, 0)),
        ],
        out_specs=pl.BlockSpec((N_HEADS, N_SAMPLE, KV_LORA + 128), lambda i: (0, 0, 0)),
        out_shape=jax.ShapeDtypeStruct((N_HEADS, N_SAMPLE, KV_LORA + 128), F32),
        name="sample_q_absorb",
    )(q, w_ukt)


def _page_copies(layer, pt_ref, lat_hbm, kr_hbm, lat_buf, kr_buf, sem, seq, slot, page):
    pid = pt_ref[seq * N_PAGES + page]
    rows = pl.ds(pl.multiple_of(page * PAGE_SIZE, PAGE_SIZE), PAGE_SIZE)
    return (
        pltpu.make_async_copy(lat_hbm.at[layer, pid], lat_buf.at[slot, rows], sem.at[0, slot]),
        pltpu.make_async_copy(kr_hbm.at[layer, pid], kr_buf.at[slot, rows], sem.at[1, slot]),
    )


def _sample_attn_kernel(pt_ref, qa_ref, cn_ref, krn_ref, lat_hbm, kr_hbm, o_ref,
                        lat_buf, kr_buf, sem, *, layer):
    n = pl.program_id(0)
    slot = lax.rem(n, 2)
    copies = functools.partial(_page_copies, layer, pt_ref, lat_hbm, kr_hbm, lat_buf, kr_buf, sem)

    def start_pages(seq, slot_):
        def body(page, _):
            for cp in copies(seq, slot_, page):
                cp.start()
            return 0
        lax.fori_loop(0, N_PAGES, body, 0)

    @pl.when(n == 0)
    def _():
        start_pages(0, 0)

    @pl.when(n + 1 < pl.num_programs(0))
    def _():
        start_pages(n + 1, 1 - slot)

    def wait_body(page, _):
        for cp in copies(n, slot, page):
            cp.wait()
        return 0
    lax.fori_loop(0, N_PAGES, wait_body, 0)

    qa = qa_ref[0]
    ql = qa[:, :KV_LORA].astype(BF16)
    qr = qa[:, KV_LORA:KV_LORA + QK_ROPE].astype(BF16)
    lat = lat_buf[slot].astype(BF16)
    kr = kr_buf[slot].astype(BF16)
    s = (_dot_nt(ql, lat) + _dot_nt(qr, kr)) * ATTN_SCALE
    cn = cn_ref[pl.ds(n, 1), :]
    krn = krn_ref[pl.ds(n, 1), :]
    s_new = (jnp.sum(ql.astype(F32) * cn.astype(BF16).astype(F32), axis=-1, keepdims=True)
             + jnp.sum(qr.astype(F32) * krn.astype(BF16).astype(F32), axis=-1, keepdims=True)) * ATTN_SCALE
    m = jnp.maximum(jnp.max(s, axis=-1, keepdims=True), s_new)
    p = jnp.exp(s - m)
    p_new = jnp.exp(s_new - m)
    inv_l = 1.0 / (jnp.sum(p, axis=-1, keepdims=True) + p_new)
    o_ref[0] = _dot((p * inv_l).astype(BF16), lat) + (p_new * inv_l) * cn


def _sample_attention(page_table, qa_t, lat, kr, lat_pool, kr_pool, layer):
    blk = N_PROMPT // N_SAMPLE
    grid_spec = pltpu.PrefetchScalarGridSpec(
        num_scalar_prefetch=1,
        grid=(N_SAMPLE,),
        in_specs=[
            pl.BlockSpec((1, N_HEADS, KV_LORA + 128), lambda n, pt: (n, 0, 0)),
            pl.BlockSpec((N_SAMPLE, KV_LORA), lambda n, pt: (blk, 0)),
            pl.BlockSpec((N_SAMPLE, QK_ROPE), lambda n, pt: (blk, 0)),
            pl.BlockSpec(memory_space=pl.ANY),
            pl.BlockSpec(memory_space=pl.ANY),
        ],
        out_specs=pl.BlockSpec((1, N_HEADS, KV_LORA), lambda n, pt: (n, 0, 0)),
        scratch_shapes=[
            pltpu.VMEM((2, PAST_LEN, KV_LORA), F32),
            pltpu.VMEM((2, PAST_LEN, QK_ROPE), F32),
            pltpu.SemaphoreType.DMA((2, 2)),
        ],
    )
    return pl.pallas_call(
        functools.partial(_sample_attn_kernel, layer=layer),
        grid_spec=grid_spec,
        out_shape=jax.ShapeDtypeStruct((N_SAMPLE, N_HEADS, KV_LORA), F32),
        compiler_params=_params(),
        name="sample_attention",
    )(page_table, qa_t, lat, kr, lat_pool, kr_pool)


def _sample_oup_kernel(ol_ref, wuv_ref, o_ref):
    o_ref[...] = jnp.zeros(o_ref.shape, BF16)
    for hd in range(N_HEADS):
        o = _dot(ol_ref[hd].astype(BF16), wuv_ref[hd])
        o_ref[0:N_SAMPLE, hd * V_HEAD:(hd + 1) * V_HEAD] = o.astype(BF16)


def _sample_oup(ol_t, w_uvh):
    return pl.pallas_call(
        _sample_oup_kernel,
        grid=(1,),
        in_specs=[
            pl.BlockSpec((N_HEADS, N_SAMPLE, KV_LORA), lambda i: (0, 0, 0)),
            pl.BlockSpec((N_HEADS, KV_LORA, V_HEAD), lambda i: (0, 0, 0)),
        ],
        out_specs=pl.BlockSpec((TOKEN_BLOCK, N_HEADS * V_HEAD), lambda i: (0, 0)),
        out_shape=jax.ShapeDtypeStruct((TOKEN_BLOCK, N_HEADS * V_HEAD), BF16),
        name="sample_o_up",
    )(ol_t, w_uvh)


def _attn_out_kernel(x_ref, op_ref, os_ref, wo_ref, g_ref, wr_ref, wrlo_ref, x1_ref, h2_ref, route_ref):
    tb = TOKEN_BLOCK
    is_sample = pl.program_id(0) == SAMPLE_BLOCK
    o = jnp.where(is_sample, os_ref[...], op_ref[...])
    x1 = x_ref[...] + _dot(o, wo_ref[...])
    x1_ref[...] = x1
    h2 = _rms_norm(x1, g_ref[...])
    h2_ref[...] = h2
    logits = _dot_split(h2, wr_ref, wrlo_ref)
    lane = lax.broadcasted_iota(jnp.int32, (tb, 128), 1)
    lane_f = lane.astype(F32)
    lg = jnp.where(lane < N_EXPERTS, logits, -jnp.inf)
    v1 = jnp.max(lg, axis=-1, keepdims=True)
    i1 = jnp.min(jnp.where(lg == v1, lane_f, 128.0), axis=-1, keepdims=True)
    lg2 = jnp.where(lane_f == i1, -jnp.inf, lg)
    v2 = jnp.max(lg2, axis=-1, keepdims=True)
    i2 = jnp.min(jnp.where(lg2 == v2, lane_f, 128.0), axis=-1, keepdims=True)
    e2 = jnp.exp(v2 - v1)
    den = 1.0 + e2
    g1 = 1.0 / den
    g2 = e2 / den
    route_ref[...] = jnp.where(lane == 0, i1, jnp.where(lane == 1, i2,
                               jnp.where(lane == 2, g1, jnp.where(lane == 3, g2, 0.0))))


def _attn_out(x, o_prompt, o_sample, w_o, norm_g, w_router):
    tb = TOKEN_BLOCK
    return pl.pallas_call(
        _attn_out_kernel,
        grid=(N_BLOCKS,),
        in_specs=[
            pl.BlockSpec((tb, D_MODEL), lambda i: (i, 0)),
            pl.BlockSpec((tb, D_MODEL), lambda i: (jnp.minimum(i, SAMPLE_BLOCK - 1), 0)),
            pl.BlockSpec((tb, D_MODEL), lambda i: (0, 0)),
            _resident((N_HEADS * V_HEAD, D_MODEL)),
            _resident((1, D_MODEL)),
            _resident((D_MODEL, 128)),
            _resident((D_MODEL, 128)),
        ],
        out_specs=[
            pl.BlockSpec((tb, D_MODEL), lambda i: (i, 0)),
            pl.BlockSpec((tb, D_MODEL), lambda i: (i, 0)),
            pl.BlockSpec((tb, 128), lambda i: (i, 0)),
        ],
        out_shape=[
            jax.ShapeDtypeStruct((T_PAD, D_MODEL), F32),
            jax.ShapeDtypeStruct((T_PAD, D_MODEL), F32),
            jax.ShapeDtypeStruct((T_PAD, 128), F32),
        ],
        compiler_params=_params(),
        name="attn_out_router",
    )(x, o_prompt, o_sample, w_o, norm_g, *_split_weights(w_router))


def _moe_kernel(texp_ref, nused_ref, src_ref, dst_ref,
                h_hbm, gate_ref, wg_ref, wu_ref, wd_ref, y_hbm,
                xg, obuf, gsem, ssem, zsem):
    tm = MOE_TM
    n_chunks = D_FF_EXPERT // MOE_FC
    i = pl.program_id(0)
    slot = lax.rem(i, 2)
    other = 1 - slot

    def gather_row(tile, slot_, r):
        idx = src_ref[tile * tm + r]
        return pltpu.make_async_copy(h_hbm.at[pl.ds(idx, 1)], xg.at[slot_, pl.ds(r, 1)], gsem.at[slot_])

    def scatter_row(tile, slot_, r):
        d = dst_ref[(tile + 1) * tm + r]
        return pltpu.make_async_copy(obuf.at[slot_, pl.ds(r, 1)], y_hbm.at[pl.ds(d, 1)], ssem.at[slot_])

    def gather_tile(slot_):
        return pltpu.make_async_copy(h_hbm.at[pl.ds(0, tm)], xg.at[slot_], gsem.at[slot_])

    def scatter_tile(slot_):
        return pltpu.make_async_copy(obuf.at[slot_], y_hbm.at[pl.ds(0, tm)], ssem.at[slot_])

    def for_rows(fn):
        def body(r, _):
            fn(r)
            return 0
        lax.fori_loop(0, tm, body, 0, unroll=8)

    @pl.when(i == 0)
    def _():
        obuf[...] = jnp.zeros(obuf.shape, F32)
        n_pad = T_PAD - N_TOK
        for first, count in ((N_TOK, n_pad), (T_PAD + N_TOK, n_pad), (2 * T_PAD, MOE_TRASH_ROWS)):
            for off in range(0, count, tm):
                n = min(tm, count - off)
                cp = pltpu.make_async_copy(obuf.at[0, pl.ds(0, n)], y_hbm.at[pl.ds(first + off, n)], zsem)
                cp.start()
                cp.wait()
        for_rows(lambda r: gather_row(0, 0, r).start())

    gather_tile(slot).wait()

    @pl.when(i < nused_ref[0])
    def _():
        x = xg[slot].astype(BF16)
        acc = jnp.zeros((tm, D_MODEL), F32)
        for c in range(n_chunks):
            cols = slice(c * MOE_FC, (c + 1) * MOE_FC)
            a = (_silu(_dot(x, wg_ref[0, :, cols])) * _dot(x, wu_ref[0, :, cols])).astype(BF16)
            acc = acc + _dot(a, wd_ref[0, cols, :])
            for r in range(c * tm // n_chunks, (c + 1) * tm // n_chunks):
                gather_row(i + 1, other, r).start()
                scatter_row(i - 1, other, r).start()
        obuf[slot] = acc * gate_ref[...]

    @pl.when(i >= nused_ref[0])
    def _():
        for_rows(lambda r: gather_row(i + 1, other, r).start())
        for_rows(lambda r: scatter_row(i - 1, other, r).start())

    scatter_tile(other).wait()

    @pl.when(i == pl.num_programs(0) - 1)
    def _():
        for_rows(lambda r: scatter_row(i, slot, r).start())
        scatter_tile(slot).wait()
        gather_tile(other).wait()


def _moe_layer(h2, tile_expert, n_used, src, dst, gate_sorted, w_gate, w_up, w_down):
    tm = MOE_TM
    grid_spec = pltpu.PrefetchScalarGridSpec(
        num_scalar_prefetch=4,
        grid=(MOE_TILES,),
        in_specs=[
            pl.BlockSpec(memory_space=pl.ANY),
            pl.BlockSpec((tm, 1), lambda i, te, nu, s, d: (i, 0)),
            pl.BlockSpec((1, D_MODEL, D_FF_EXPERT), lambda i, te, nu, s, d: (te[i], 0, 0)),
            pl.BlockSpec((1, D_MODEL, D_FF_EXPERT), lambda i, te, nu, s, d: (te[i], 0, 0)),
            pl.BlockSpec((1, D_FF_EXPERT, D_MODEL), lambda i, te, nu, s, d: (te[i], 0, 0)),
        ],
        out_specs=pl.BlockSpec(memory_space=pl.ANY),
        scratch_shapes=[
            pltpu.VMEM((2, tm, D_MODEL), F32),
            pltpu.VMEM((2, tm, D_MODEL), F32),
            pltpu.SemaphoreType.DMA((2,)),
            pltpu.SemaphoreType.DMA((2,)),
            pltpu.SemaphoreType.DMA(()),
        ],
    )
    return pl.pallas_call(
        _moe_kernel,
        grid_spec=grid_spec,
        out_shape=jax.ShapeDtypeStruct((Y_ROWS, D_MODEL), F32),
        compiler_params=_params(),
        name="moe_experts",
    )(tile_expert, n_used, src, dst, h2, gate_sorted, w_gate, w_up, w_down)


def _moe_plan(route):
    tm = MOE_TM
    r = route[:N_TOK]
    expert = jnp.concatenate([r[:, 0], r[:, 1]]).astype(jnp.int32)
    gate = jnp.concatenate([r[:, 2], r[:, 3]])
    tok = jnp.arange(N_TOK, dtype=jnp.int32)
    flat = jnp.concatenate([tok, T_PAD + tok])
    onehot = (expert[:, None] == jnp.arange(N_EXPERTS, dtype=jnp.int32)[None, :]).astype(jnp.int32)
    csum = jnp.cumsum(onehot, axis=0)
    counts = csum[-1]
    rank = jnp.sum(csum * onehot, axis=1) - 1
    tiles = (counts + tm - 1) // tm
    tile_end = jnp.cumsum(tiles)
    tile_start = tile_end - tiles
    pos = tile_start[expert] * tm + rank
    n_used = tile_end[-1]
    table = jnp.full((MOE_ROWS, 2), -1, jnp.int32).at[pos].set(
        jnp.stack([flat, lax.bitcast_convert_type(gate, jnp.int32)], axis=1))
    dst = table[:, 0]
    is_pad = dst < 0
    gate_sorted = jnp.where(is_pad, 0.0, lax.bitcast_convert_type(table[:, 1], F32))
    src = jnp.where(is_pad, 0, dst % T_PAD)
    pad_rank = jnp.cumsum(is_pad.astype(jnp.int32)) - 1
    dst = jnp.where(is_pad, 2 * T_PAD + pad_rank, dst)
    src = jnp.concatenate([src, jnp.zeros((tm,), jnp.int32)])
    dst = jnp.concatenate([2 * T_PAD + MOE_PAD_ROWS + jnp.arange(tm, dtype=jnp.int32), dst])
    tile_id = jnp.arange(MOE_TILES, dtype=jnp.int32)
    texp = jnp.sum((tile_id[:, None] >= tile_end[None, :]).astype(jnp.int32), axis=1)
    last = jnp.sum((n_used - 1 >= tile_end).astype(jnp.int32))
    texp = jnp.minimum(jnp.where(tile_id < n_used, texp, last), N_EXPERTS - 1)
    return texp, n_used.reshape(1), src, dst, gate_sorted.reshape(MOE_ROWS, 1)


def _combine_kernel(x_ref, y0_ref, y1_ref, o_ref):
    o_ref[...] = x_ref[...] + y0_ref[...] + y1_ref[...]


def _combine_norm_kernel(x_ref, y0_ref, y1_ref, g_ref, o_ref):
    o_ref[...] = _rms_norm(x_ref[...] + y0_ref[...] + y1_ref[...], g_ref[...])


def _combine(x1, y, final_g=None):
    tb = TOKEN_BLOCK
    in_specs = [
        pl.BlockSpec((tb, D_MODEL), lambda i: (i, 0)),
        pl.BlockSpec((tb, D_MODEL), lambda i: (i, 0)),
        pl.BlockSpec((tb, D_MODEL), lambda i: (N_BLOCKS + i, 0)),
    ]
    args = [x1, y, y]
    if final_g is not None:
        in_specs.append(_resident((1, D_MODEL)))
        args.append(final_g)
    return pl.pallas_call(
        _combine_kernel if final_g is None else _combine_norm_kernel,
        grid=(N_BLOCKS,),
        in_specs=in_specs,
        out_specs=pl.BlockSpec((tb, D_MODEL), lambda i: (i, 0)),
        out_shape=jax.ShapeDtypeStruct((T_PAD, D_MODEL), F32),
        name="moe_combine",
    )(*args)


def _swap_halves(w):
    half = w.shape[-1] // 2
    return jnp.concatenate([w[..., half:], w[..., :half]], axis=-1)


def _pad_last(w, n):
    return jnp.pad(w, [(0, 0)] * (w.ndim - 1) + [(0, n - w.shape[-1])])


def _rope_tables():
    half = QK_ROPE // 2
    pos = jnp.concatenate([
        jnp.tile(jnp.arange(SEQ, dtype=jnp.int32), BATCH),
        jnp.full((N_SAMPLE,), PAST_LEN, jnp.int32),
        jnp.zeros((T_PAD - N_TOK,), jnp.int32),
    ])
    inv_freq = ROPE_THETA ** (-jnp.arange(half, dtype=F32) / half)
    ang = pos.astype(F32)[:, None] * inv_freq[None, :]
    cos, sin = jnp.cos(ang), jnp.sin(ang)
    return (jnp.concatenate([cos, cos, cos, cos], axis=-1),
            jnp.concatenate([-sin, sin, -sin, sin], axis=-1))


def _sgu_spatial_params(w_s, b_s):
    causal = jnp.tril(jnp.ones((CHUNK, CHUNK), dtype=bool))
    ws_causal = jnp.where(causal[None], w_s, 0.0).astype(BF16)
    bias_full = jnp.repeat(b_s.T, CHUNK, axis=1)
    ws0_row = jnp.repeat(w_s[:, 0, 0], CHUNK)[None, :]
    bias0_row = jnp.repeat(b_s[:, 0], CHUNK)[None, :]
    return ws_causal, bias_full, ws0_row, bias0_row


def _mla_params(w_dq, w_uq, w_dkv, w_uk, w_uv):
    qn = w_uq[:, :, :QK_NOPE].reshape(Q_LORA, N_HEADS * QK_NOPE)
    qr = w_uq[:, :, QK_NOPE:]
    qr_pad = _pad_last(qr, 128).reshape(Q_LORA, N_HEADS * 128)
    qrs_pad = _pad_last(_swap_halves(qr), 128).reshape(Q_LORA, N_HEADS * 128)
    w_q = jnp.concatenate([qn, qr_pad, qrs_pad], axis=1).astype(BF16)
    kr = w_dkv[:, KV_LORA:]
    w_dkv3 = jnp.concatenate(
        [w_dkv[:, :KV_LORA], _pad_last(kr, 128), _pad_last(_swap_halves(kr), 128)], axis=1).astype(BF16)
    w_ukv = jnp.concatenate([w_uk.reshape(KV_LORA, -1), w_uv.reshape(KV_LORA, -1)], axis=1).astype(BF16)
    w_ukt = w_uk.transpose(1, 2, 0).astype(BF16)
    w_uvh = w_uv.transpose(1, 0, 2).astype(BF16)
    return w_dq.astype(BF16), w_q, w_dkv3, w_ukv, w_ukt, w_uvh


def kernel(x_prompt, x_sample, cache_latent, cache_k_rope, page_table, norm_mix_g, norm_ffn_g, final_norm_g, sgu_w_in, sgu_ln_g, sgu_ln_b, sgu_w_s, sgu_b_s, sgu_w_out, mla_w_dq, mla_q_norm_g, mla_w_uq, mla_w_dkv, mla_kv_norm_g, mla_w_uk, mla_w_uv, mla_w_o, ffn_w_gate, ffn_w_up, ffn_w_down, moe_w_router, moe_w_gate, moe_w_up, moe_w_down):
    x = jnp.concatenate([
        x_prompt.reshape(N_PROMPT, D_MODEL),
        x_sample.reshape(N_SAMPLE, D_MODEL),
        jnp.zeros((T_PAD - N_TOK, D_MODEL), F32),
    ])
    cos_t, sin_t = _rope_tables()
    pt_flat = page_table.reshape(-1)
    lat_all, kr_all, v_s = [], [], []
    y = None
    for i in range(DEPTH):
        j = i // 2
        g_mix = norm_mix_g[i].reshape(1, D_MODEL)
        g_ffn = norm_ffn_g[i].reshape(1, D_MODEL)
        if i % 2 == 0:
            x, vs = _sgu_layer(x, g_mix, sgu_w_in[j], sgu_ln_g[j].reshape(1, -1),
                               sgu_ln_b[j].reshape(1, -1), *_sgu_spatial_params(sgu_w_s[j], sgu_b_s[j]),
                               sgu_w_out[j])
            v_s.append(vs.reshape(N_SAMPLE, 1, D_SGU))
            w_gu = jnp.concatenate([ffn_w_gate[j], ffn_w_up[j]], axis=1)
            x = _ffn_layer(x, g_ffn, w_gu, ffn_w_down[j])
        else:
            w_dq, w_q, w_dkv3, w_ukv, w_ukt, w_uvh = _mla_params(
                mla_w_dq[j], mla_w_uq[j], mla_w_dkv[j], mla_w_uk[j], mla_w_uv[j])
            q, k, v, lat, kr = _mla_proj(x, g_mix, w_dq, mla_q_norm_g[j].reshape(1, -1), w_q, w_dkv3,
                                         mla_kv_norm_g[j].reshape(1, -1), w_ukv, cos_t, sin_t)
            lat_all.append(lat)
            kr_all.append(kr)
            o_prompt = _flash_attention(q, k, v)
            qa = _sample_qabs(q, w_ukt)
            o_lat = _sample_attention(pt_flat, qa.transpose(1, 0, 2), lat, kr,
                                      cache_latent, cache_k_rope, layer=j)
            o_sample = _sample_oup(o_lat.transpose(1, 0, 2), w_uvh)
            w_r = _pad_last(moe_w_router[j], 128)
            x1, h2, route = _attn_out(x, o_prompt, o_sample, mla_w_o[j].astype(BF16), g_ffn, w_r)
            texp, n_used, src, dst, gate_sorted = _moe_plan(route)
            y = _moe_layer(h2, texp, n_used, src, dst, gate_sorted, moe_w_gate[j].astype(BF16),
                           moe_w_up[j].astype(BF16), moe_w_down[j].astype(BF16))
            final_g = final_norm_g.reshape(1, D_MODEL) if i == DEPTH - 1 else None
            x = _combine(x1, y, final_g)

    def split(a, width):
        return (a[:N_PROMPT].reshape(BATCH, SEQ, width), a[N_PROMPT:N_TOK].reshape(N_SAMPLE, 1, width))

    y_prompt, y_sample = split(x, D_MODEL)
    lat_p, lat_s = zip(*[split(a, KV_LORA) for a in lat_all])
    kr_p, kr_s = zip(*[split(a, QK_ROPE) for a in kr_all])
    return (y_prompt, y_sample, jnp.stack(lat_p), jnp.stack(kr_p), jnp.stack(lat_s), jnp.stack(kr_s),
            jnp.stack(v_s))
```

```python
import functools
import math

import jax
import jax.numpy as jnp
from jax import lax
from jax.experimental import pallas as pl
from jax.experimental.pallas import tpu as pltpu

F32 = jnp.float32
BF16 = jnp.bfloat16

D_MODEL = 1024
BATCH = 2
SEQ = 8192
DEPTH = 4
N_SAMPLE = 128
PAST_LEN = 8192
PAGE_SIZE = 128
N_PAGES = PAST_LEN // PAGE_SIZE

CHUNK = 128
D_SGU = 2 * D_MODEL
N_SGU_GROUPS = D_SGU // CHUNK

N_HEADS = 8
QK_NOPE = 128
QK_ROPE = 64
V_HEAD = 128
Q_LORA = 384
KV_LORA = 256
ROPE_THETA = 10000.0
ATTN_SCALE = 1.0 / math.sqrt(QK_NOPE + QK_ROPE)
QK_PAD = 256

D_FF = 5 * D_MODEL // 2
N_EXPERTS = 8
D_FF_EXPERT = 7 * D_MODEL // 2
NORM_EPS = 1e-6
LN_EPS = 1e-5
NEG_INF = -1e30

TOKEN_BLOCK = 512
N_PROMPT = BATCH * SEQ
N_TOK = N_PROMPT + N_SAMPLE
N_BLOCKS = pl.cdiv(N_TOK, TOKEN_BLOCK)
T_PAD = N_BLOCKS * TOKEN_BLOCK
SAMPLE_BLOCK = N_PROMPT // TOKEN_BLOCK
CHUNKS_PER_BLOCK = TOKEN_BLOCK // CHUNK

ATTN_TQ = 1024
ATTN_TK = 1024
ATTN_HEADS = 1

MOE_TM = 256
MOE_TILES = 2 * N_TOK // MOE_TM + N_EXPERTS
MOE_ROWS = MOE_TILES * MOE_TM
MOE_FC = 512
MOE_PAD_ROWS = MOE_ROWS - 2 * N_TOK
MOE_TRASH_ROWS = MOE_PAD_ROWS + MOE_TM
Y_ROWS = 2 * T_PAD + MOE_TRASH_ROWS

VMEM_LIMIT = 56 * 1024 * 1024


def _params(**kw):
    return pltpu.CompilerParams(vmem_limit_bytes=VMEM_LIMIT, **kw)


def _resident(shape):
    return pl.BlockSpec(shape, lambda *_: (0,) * len(shape), pipeline_mode=pl.Buffered(1))


def _rms_norm(x, g):
    return x * lax.rsqrt(jnp.mean(x * x, axis=-1, keepdims=True) + NORM_EPS) * g


def _gelu(x):
    return 0.5 * x * (1.0 + lax.erf(x * math.sqrt(0.5)))


def _silu(x):
    return x * (1.0 / (1.0 + jnp.exp(-x)))


def _dot(a, b):
    return jnp.dot(a, b, preferred_element_type=F32)


def _split_bf16(w):
    hi = w.astype(BF16)
    return hi, (w - hi.astype(F32)).astype(BF16)


def _split_weights_kernel(w_ref, hi_ref, lo_ref):
    hi_ref[...], lo_ref[...] = _split_bf16(w_ref[...])


def _split_weights(w):
    rows, cols = w.shape
    br = 256
    spec = pl.BlockSpec((br, cols), lambda i: (i, 0))
    return pl.pallas_call(
        _split_weights_kernel,
        grid=(rows // br,),
        in_specs=[spec],
        out_specs=[spec, spec],
        out_shape=[jax.ShapeDtypeStruct(w.shape, BF16)] * 2,
        name="split_weights",
    )(w)


def _dot_split(a, w_hi_ref, w_lo_ref):
    a_hi, a_lo = _split_bf16(a)
    w_hi = w_hi_ref[...]
    return _dot(a_hi, w_hi) + (_dot(a_lo, w_hi) + _dot(a_hi, w_lo_ref[...]))


def _dot_nt(a, b):
    return lax.dot_general(a, b, (((1,), (1,)), ((), ())), preferred_element_type=F32)


def _sgu_gelu_ln(z, lng_ref, lnb_ref):
    z = _gelu(z)
    u = z[:, :D_SGU]
    v = z[:, D_SGU:]
    vc = v - jnp.mean(v, axis=-1, keepdims=True)
    vn = vc * lax.rsqrt(jnp.mean(vc * vc, axis=-1, keepdims=True) + LN_EPS)
    return u, vn * lng_ref[...] + lnb_ref[...]


def _sgu_kernel(x_ref, g_ref, win_ref, winlo_ref, lng_ref, lnb_ref, ws_ref, bias_ref, ws0_ref, bias0_ref,
                wout_ref, woutlo_ref, o_ref, v_ref, gated_ref):
    is_sample = pl.program_id(0) == SAMPLE_BLOCK

    @pl.when(is_sample)
    def _():
        x = x_ref[0:N_SAMPLE, :]
        h = _rms_norm(x, g_ref[...])
        u, vn = _sgu_gelu_ln(_dot_split(h, win_ref, winlo_ref), lng_ref, lnb_ref)
        v_ref[...] = vn
        gated = u * (vn * ws0_ref[...] + bias0_ref[...])
        o_ref[0:N_SAMPLE, :] = x + _dot_split(gated, wout_ref, woutlo_ref)
        o_ref[N_SAMPLE:, :] = jnp.zeros((TOKEN_BLOCK - N_SAMPLE, D_MODEL), F32)

    @pl.when(jnp.logical_not(is_sample))
    def _():
        x = x_ref[...]
        h = _rms_norm(x, g_ref[...]).astype(BF16)
        u, vn = _sgu_gelu_ln(_dot(h, win_ref[...]), lng_ref, lnb_ref)
        vb = vn.astype(BF16)
        for g in range(N_SGU_GROUPS):
            cols = slice(g * CHUNK, (g + 1) * CHUNK)
            rhs = jnp.concatenate(
                [vb[c * CHUNK:(c + 1) * CHUNK, cols] for c in range(CHUNKS_PER_BLOCK)], axis=1)
            mixed = _dot(ws_ref[g], rhs)
            for c in range(CHUNKS_PER_BLOCK):
                rows = slice(c * CHUNK, (c + 1) * CHUNK)
                m = mixed[:, c * CHUNK:(c + 1) * CHUNK] + bias_ref[:, cols]
                gated_ref[rows, cols] = (u[rows, cols] * m).astype(BF16)
        o_ref[...] = x + _dot(gated_ref[...], wout_ref[...])


def _sgu_layer(x, norm_g, w_in, ln_g, ln_b, ws_causal, bias_full, ws0_row, bias0_row, w_out):
    w_in, w_in_lo = _split_weights(w_in)
    w_out, w_out_lo = _split_weights(w_out)
    tb = TOKEN_BLOCK
    return pl.pallas_call(
        _sgu_kernel,
        grid=(N_BLOCKS,),
        in_specs=[
            pl.BlockSpec((tb, D_MODEL), lambda i: (i, 0)),
            _resident((1, D_MODEL)),
            _resident((D_MODEL, 2 * D_SGU)),
            _resident((D_MODEL, 2 * D_SGU)),
            _resident((1, D_SGU)),
            _resident((1, D_SGU)),
            _resident((N_SGU_GROUPS, CHUNK, CHUNK)),
            _resident((CHUNK, D_SGU)),
            _resident((1, D_SGU)),
            _resident((1, D_SGU)),
            _resident((D_SGU, D_MODEL)),
            _resident((D_SGU, D_MODEL)),
        ],
        out_specs=[
            pl.BlockSpec((tb, D_MODEL), lambda i: (i, 0)),
            pl.BlockSpec((N_SAMPLE, D_SGU), lambda i: (0, 0)),
        ],
        out_shape=[
            jax.ShapeDtypeStruct((T_PAD, D_MODEL), F32),
            jax.ShapeDtypeStruct((N_SAMPLE, D_SGU), F32),
        ],
        scratch_shapes=[pltpu.VMEM((tb, D_SGU), BF16)],
        compiler_params=_params(),
        name="sgu_mixer",
    )(x, norm_g, w_in, w_in_lo, ln_g, ln_b, ws_causal, bias_full, ws0_row, bias0_row, w_out, w_out_lo)


def _ffn_kernel(x_ref, g_ref, wgu_ref, wgulo_ref, wd_ref, wdlo_ref, o_ref):
    is_sample = pl.program_id(0) == SAMPLE_BLOCK

    @pl.when(is_sample)
    def _():
        x = x_ref[0:N_SAMPLE, :]
        gu = _dot_split(_rms_norm(x, g_ref[...]), wgu_ref, wgulo_ref)
        a = _silu(gu[:, :D_FF]) * gu[:, D_FF:]
        o_ref[0:N_SAMPLE, :] = x + _dot_split(a, wd_ref, wdlo_ref)
        o_ref[N_SAMPLE:, :] = jnp.zeros((TOKEN_BLOCK - N_SAMPLE, D_MODEL), F32)

    @pl.when(jnp.logical_not(is_sample))
    def _():
        x = x_ref[...]
        h = _rms_norm(x, g_ref[...]).astype(BF16)
        gu = _dot(h, wgu_ref[...])
        a = (_silu(gu[:, :D_FF]) * gu[:, D_FF:]).astype(BF16)
        o_ref[...] = x + _dot(a, wd_ref[...])


def _ffn_layer(x, norm_g, w_gu, w_down):
    tb = TOKEN_BLOCK
    w_gu, w_gu_lo = _split_weights(w_gu)
    w_down, w_down_lo = _split_weights(w_down)
    return pl.pallas_call(
        _ffn_kernel,
        grid=(N_BLOCKS,),
        in_specs=[
            pl.BlockSpec((tb, D_MODEL), lambda i: (i, 0)),
            _resident((1, D_MODEL)),
            _resident((D_MODEL, 2 * D_FF)),
            _resident((D_MODEL, 2 * D_FF)),
            _resident((D_FF, D_MODEL)),
            _resident((D_FF, D_MODEL)),
        ],
        out_specs=pl.BlockSpec((tb, D_MODEL), lambda i: (i, 0)),
        out_shape=jax.ShapeDtypeStruct((T_PAD, D_MODEL), F32),
        compiler_params=_params(),
        name="dense_ffn",
    )(x, norm_g, w_gu, w_gu_lo, w_down, w_down_lo)


def _mla_proj_kernel(x_ref, g_ref, wdq_ref, qg_ref, wq_ref, wdkv_ref, kvg_ref, wukv_ref,
                     cos_ref, sin_ref, q_ref, k_ref, v_ref, lat_ref, kr_ref):
    hd_w = N_HEADS * QK_NOPE
    x = x_ref[...]
    h = _rms_norm(x, g_ref[...]).astype(BF16)
    cq = _rms_norm(_dot(h, wdq_ref[...]), qg_ref[...]).astype(BF16)
    q = _dot(cq, wq_ref[...])
    cos = cos_ref[...]
    sin = sin_ref[...]
    ckv = _dot(h, wdkv_ref[...])
    c = _rms_norm(ckv[:, :KV_LORA], kvg_ref[...])
    lat_ref[...] = c
    kr = ckv[:, KV_LORA:KV_LORA + 128] * cos + ckv[:, KV_LORA + 128:] * sin
    kr_ref[...] = kr[:, :QK_ROPE]
    krb = kr.astype(BF16)
    kv = _dot(c.astype(BF16), wukv_ref[...])
    for hd in range(N_HEADS):
        lo, hi = hd * 128, (hd + 1) * 128
        qr = q[:, hd_w + lo:hd_w + hi] * cos + q[:, 2 * hd_w + lo:2 * hd_w + hi] * sin
        q_ref[hd, :, 0:128] = q[:, lo:hi].astype(BF16)
        q_ref[hd, :, 128:256] = qr.astype(BF16)
        k_ref[hd, :, 0:128] = kv[:, lo:hi].astype(BF16)
        k_ref[hd, :, 128:256] = krb
        v_ref[hd] = kv[:, hd_w + lo:hd_w + hi].astype(BF16)


def _mla_proj(x, norm_g, w_dq, q_norm_g, w_q, w_dkv, kv_norm_g, w_ukv, cos_t, sin_t):
    tb = TOKEN_BLOCK
    return pl.pallas_call(
        _mla_proj_kernel,
        grid=(N_BLOCKS,),
        in_specs=[
            pl.BlockSpec((tb, D_MODEL), lambda i: (i, 0)),
            _resident((1, D_MODEL)),
            _resident((D_MODEL, Q_LORA)),
            _resident((1, Q_LORA)),
            _resident((Q_LORA, 3 * N_HEADS * 128)),
            _resident((D_MODEL, KV_LORA + 256)),
            _resident((1, KV_LORA)),
            _resident((KV_LORA, 2 * N_HEADS * 128)),
            pl.BlockSpec((tb, 128), lambda i: (i, 0)),
            pl.BlockSpec((tb, 128), lambda i: (i, 0)),
        ],
        out_specs=[
            pl.BlockSpec((N_HEADS, tb, QK_PAD), lambda i: (0, i, 0)),
            pl.BlockSpec((N_HEADS, tb, QK_PAD), lambda i: (0, i, 0)),
            pl.BlockSpec((N_HEADS, tb, V_HEAD), lambda i: (0, i, 0)),
            pl.BlockSpec((tb, KV_LORA), lambda i: (i, 0)),
            pl.BlockSpec((tb, QK_ROPE), lambda i: (i, 0)),
        ],
        out_shape=[
            jax.ShapeDtypeStruct((N_HEADS, T_PAD, QK_PAD), BF16),
            jax.ShapeDtypeStruct((N_HEADS, T_PAD, QK_PAD), BF16),
            jax.ShapeDtypeStruct((N_HEADS, T_PAD, V_HEAD), BF16),
            jax.ShapeDtypeStruct((T_PAD, KV_LORA), F32),
            jax.ShapeDtypeStruct((T_PAD, QK_ROPE), F32),
        ],
        compiler_params=_params(),
        name="mla_proj",
    )(x, norm_g, w_dq, q_norm_g, w_q, w_dkv, kv_norm_g, w_ukv, cos_t, sin_t)


def _flash_kernel(q_ref, k_ref, v_ref, o_ref):
    tq, tk = ATTN_TQ, ATTN_TK
    qi = pl.program_id(2)
    c = ATTN_SCALE * math.log2(math.e)

    def step(j, carry, masked):
        start = pl.multiple_of(j * tk, tk)
        out = []
        for hd in range(ATTN_HEADS):
            m, l, acc = carry[hd]
            k = k_ref[hd, pl.ds(start, tk), :]
            v = v_ref[hd, pl.ds(start, tk), :]
            s = _dot_nt(q_ref[hd], k)
            if masked:
                row = lax.broadcasted_iota(jnp.int32, (tq, tk), 0)
                col = lax.broadcasted_iota(jnp.int32, (tq, tk), 1)
                s = jnp.where(col <= row, s, NEG_INF)
            m_new = jnp.maximum(m, jnp.max(s, axis=-1, keepdims=True))
            mc = m_new * c
            alpha = jnp.exp2(m * c - mc)
            p = jnp.exp2(s * c - mc)
            l = alpha * l + jnp.sum(p, axis=-1, keepdims=True)
            acc = alpha * acc + _dot(p.astype(BF16), v)
            out.append((m_new, l, acc))
        return tuple(out)

    init = tuple((jnp.full((tq, 1), NEG_INF, F32), jnp.zeros((tq, 1), F32),
                  jnp.zeros((tq, V_HEAD), F32)) for _ in range(ATTN_HEADS))
    carry = lax.fori_loop(0, qi, functools.partial(step, masked=False), init)
    carry = step(qi, carry, masked=True)
    for hd in range(ATTN_HEADS):
        _, l, acc = carry[hd]
        o_ref[:, hd * V_HEAD:(hd + 1) * V_HEAD] = (acc * (1.0 / l)).astype(BF16)


def _flash_attention(q, k, v):
    assert ATTN_TQ == ATTN_TK
    nq = SEQ // ATTN_TQ
    g = ATTN_HEADS
    return pl.pallas_call(
        _flash_kernel,
        grid=(BATCH, N_HEADS // g, nq),
        in_specs=[
            pl.BlockSpec((g, ATTN_TQ, QK_PAD), lambda b, h, i: (h, b * nq + i, 0)),
            pl.BlockSpec((g, SEQ, QK_PAD), lambda b, h, i: (h, b, 0)),
            pl.BlockSpec((g, SEQ, V_HEAD), lambda b, h, i: (h, b, 0)),
        ],
        out_specs=pl.BlockSpec((ATTN_TQ, g * V_HEAD), lambda b, h, i: (b * nq + i, h)),
        out_shape=jax.ShapeDtypeStruct((N_PROMPT, N_HEADS * V_HEAD), BF16),
        compiler_params=_params(),
        name="flash_attention",
    )(q, k, v)


def _sample_qabs_kernel(q_ref, wukt_ref, qa_ref):
    for hd in range(N_HEADS):
        q = q_ref[hd]
        qa_ref[hd, :, 0:KV_LORA] = _dot(q[:, :QK_NOPE], wukt_ref[hd])
        qa_ref[hd, :, KV_LORA:] = q[:, QK_NOPE:].astype(F32)


def _sample_qabs(q, w_ukt):
    blk = N_PROMPT // N_SAMPLE
    return pl.pallas_call(
        _sample_qabs_kernel,
        grid=(1,),
        in_specs=[
            pl.BlockSpec((N_HEADS, N_SAMPLE, QK_PAD), lambda i: (0, blk, 0)),
            pl.BlockSpec((N_HEADS, QK_NOPE, KV_LORA), lambda i: (0, 0, 0)),
        ],
        out_specs=pl.BlockSpec((N_HEADS, N_SAMPLE, KV_LORA + 128), lambda i: (0, 0, 0)),
        out_shape=jax.ShapeDtypeStruct((N_HEADS, N_SAMPLE, KV_LORA + 128), F32),
        name="sample_q_absorb",
    )(q, w_ukt)


def _page_copies(layer, pt_ref, lat_hbm, kr_hbm, lat_buf, kr_buf, sem, seq, slot, page):
    pid = pt_ref[seq * N_PAGES + page]
    rows = pl.ds(pl.multiple_of(page * PAGE_SIZE, PAGE_SIZE), PAGE_SIZE)
    return (
        pltpu.make_async_copy(lat_hbm.at[layer, pid], lat_buf.at[slot, rows], sem.at[0, slot]),
        pltpu.make_async_copy(kr_hbm.at[layer, pid], kr_buf.at[slot, :, rows], sem.at[1, slot]),
    )


def _sample_attn_kernel(pt_ref, qa_ref, cn_ref, krn_ref, lat_hbm, kr_hbm, o_ref,
                        lat_buf, kr_buf, sem, *, layer):
    n = pl.program_id(0)
    slot = lax.rem(n, 2)
    copies = functools.partial(_page_copies, layer, pt_ref, lat_hbm, kr_hbm, lat_buf, kr_buf, sem)

    def start_pages(seq, slot_):
        def body(page, _):
            for cp in copies(seq, slot_, page):
                cp.start()
            return 0
        lax.fori_loop(0, N_PAGES, body, 0)

    @pl.when(n == 0)
    def _():
        start_pages(0, 0)

    @pl.when(n + 1 < pl.num_programs(0))
    def _():
        start_pages(n + 1, 1 - slot)

    def wait_body(page, _):
        for cp in copies(n, slot, page):
            cp.wait()
        return 0
    lax.fori_loop(0, N_PAGES, wait_body, 0)

    qa = qa_ref[0]
    ql = qa[:, :KV_LORA].astype(BF16)
    qr = qa[:, KV_LORA:KV_LORA + QK_ROPE].astype(BF16)
    lat = lat_buf[slot].astype(BF16)
    kr_t = kr_buf[slot].astype(BF16)
    s = (_dot_nt(ql, lat) + _dot(qr, kr_t)) * ATTN_SCALE
    cn = cn_ref[pl.ds(n, 1), :]
    krn = krn_ref[pl.ds(n, 1), :]
    s_new = (jnp.sum(ql.astype(F32) * cn.astype(BF16).astype(F32), axis=-1, keepdims=True)
             + jnp.sum(qr.astype(F32) * krn.astype(BF16).astype(F32), axis=-1, keepdims=True)) * ATTN_SCALE
    m = jnp.maximum(jnp.max(s, axis=-1, keepdims=True), s_new)
    p = jnp.exp(s - m)
    p_new = jnp.exp(s_new - m)
    inv_l = 1.0 / (jnp.sum(p, axis=-1, keepdims=True) + p_new)
    o_ref[0] = _dot((p * inv_l).astype(BF16), lat) + (p_new * inv_l) * cn


def _sample_attention(page_table, qa_t, lat, kr, lat_pool, kr_pool, layer):
    blk = N_PROMPT // N_SAMPLE
    grid_spec = pltpu.PrefetchScalarGridSpec(
        num_scalar_prefetch=1,
        grid=(N_SAMPLE,),
        in_specs=[
            pl.BlockSpec((1, N_HEADS, KV_LORA + 128), lambda n, pt: (n, 0, 0)),
            pl.BlockSpec((N_SAMPLE, KV_LORA), lambda n, pt: (blk, 0)),
            pl.BlockSpec((N_SAMPLE, QK_ROPE), lambda n, pt: (blk, 0)),
            pl.BlockSpec(memory_space=pl.ANY),
            pl.BlockSpec(memory_space=pl.ANY),
        ],
        out_specs=pl.BlockSpec((1, N_HEADS, KV_LORA), lambda n, pt: (n, 0, 0)),
        scratch_shapes=[
            pltpu.VMEM((2, PAST_LEN, KV_LORA), F32),
            pltpu.VMEM((2, QK_ROPE, PAST_LEN), F32),
            pltpu.SemaphoreType.DMA((2, 2)),
        ],
    )
    return pl.pallas_call(
        functools.partial(_sample_attn_kernel, layer=layer),
        grid_spec=grid_spec,
        out_shape=jax.ShapeDtypeStruct((N_SAMPLE, N_HEADS, KV_LORA), F32),
        compiler_params=_params(),
        name="sample_attention",
    )(page_table, qa_t, lat, kr, lat_pool, kr_pool)


def _sample_oup_kernel(ol_ref, wuv_ref, o_ref):
    o_ref[...] = jnp.zeros(o_ref.shape, BF16)
    for hd in range(N_HEADS):
        o = _dot(ol_ref[hd].astype(BF16), wuv_ref[hd])
        o_ref[0:N_SAMPLE, hd * V_HEAD:(hd + 1) * V_HEAD] = o.astype(BF16)


def _sample_oup(ol_t, w_uvh):
    return pl.pallas_call(
        _sample_oup_kernel,
        grid=(1,),
        in_specs=[
            pl.BlockSpec((N_HEADS, N_SAMPLE, KV_LORA), lambda i: (0, 0, 0)),
            pl.BlockSpec((N_HEADS, KV_LORA, V_HEAD), lambda i: (0, 0, 0)),
        ],
        out_specs=pl.BlockSpec((TOKEN_BLOCK, N_HEADS * V_HEAD), lambda i: (0, 0)),
        out_shape=jax.ShapeDtypeStruct((TOKEN_BLOCK, N_HEADS * V_HEAD), BF16),
        name="sample_o_up",
    )(ol_t, w_uvh)


def _attn_out_kernel(x_ref, op_ref, os_ref, wo_ref, g_ref, wr_ref, wrlo_ref, x1_ref, h2_ref, route_ref):
    tb = TOKEN_BLOCK
    is_sample = pl.program_id(0) == SAMPLE_BLOCK
    o = jnp.where(is_sample, os_ref[...], op_ref[...])
    x1 = x_ref[...] + _dot(o, wo_ref[...])
    x1_ref[...] = x1
    h2 = _rms_norm(x1, g_ref[...])
    h2_ref[...] = h2
    logits = _dot_split(h2, wr_ref, wrlo_ref)
    lane = lax.broadcasted_iota(jnp.int32, (tb, 128), 1)
    lane_f = lane.astype(F32)
    lg = jnp.where(lane < N_EXPERTS, logits, -jnp.inf)
    v1 = jnp.max(lg, axis=-1, keepdims=True)
    i1 = jnp.min(jnp.where(lg == v1, lane_f, 128.0), axis=-1, keepdims=True)
    lg2 = jnp.where(lane_f == i1, -jnp.inf, lg)
    v2 = jnp.max(lg2, axis=-1, keepdims=True)
    i2 = jnp.min(jnp.where(lg2 == v2, lane_f, 128.0), axis=-1, keepdims=True)
    e2 = jnp.exp(v2 - v1)
    den = 1.0 + e2
    g1 = 1.0 / den
    g2 = e2 / den
    route_ref[...] = jnp.where(lane == 0, i1, jnp.where(lane == 1, i2,
                               jnp.where(lane == 2, g1, jnp.where(lane == 3, g2, 0.0))))


def _attn_out(x, o_prompt, o_sample, w_o, norm_g, w_router):
    tb = TOKEN_BLOCK
    return pl.pallas_call(
        _attn_out_kernel,
        grid=(N_BLOCKS,),
        in_specs=[
            pl.BlockSpec((tb, D_MODEL), lambda i: (i, 0)),
            pl.BlockSpec((tb, D_MODEL), lambda i: (jnp.minimum(i, SAMPLE_BLOCK - 1), 0)),
            pl.BlockSpec((tb, D_MODEL), lambda i: (0, 0)),
            _resident((N_HEADS * V_HEAD, D_MODEL)),
            _resident((1, D_MODEL)),
            _resident((D_MODEL, 128)),
            _resident((D_MODEL, 128)),
        ],
        out_specs=[
            pl.BlockSpec((tb, D_MODEL), lambda i: (i, 0)),
            pl.BlockSpec((tb, D_MODEL), lambda i: (i, 0)),
            pl.BlockSpec((tb, 128), lambda i: (i, 0)),
        ],
        out_shape=[
            jax.ShapeDtypeStruct((T_PAD, D_MODEL), F32),
            jax.ShapeDtypeStruct((T_PAD, D_MODEL), F32),
            jax.ShapeDtypeStruct((T_PAD, 128), F32),
        ],
        compiler_params=_params(),
        name="attn_out_router",
    )(x, o_prompt, o_sample, w_o, norm_g, *_split_weights(w_router))


def _moe_kernel(texp_ref, nused_ref, src_ref, dst_ref,
                h_hbm, gate_ref, wg_ref, wu_ref, wd_ref, y_hbm,
                xg, obuf, gsem, ssem, zsem):
    tm = MOE_TM
    n_chunks = D_FF_EXPERT // MOE_FC
    i = pl.program_id(0)
    slot = lax.rem(i, 2)
    other = 1 - slot

    def gather_row(tile, slot_, r):
        idx = src_ref[tile * tm + r]
        return pltpu.make_async_copy(h_hbm.at[pl.ds(idx, 1)], xg.at[slot_, pl.ds(r, 1)], gsem.at[slot_])

    def scatter_row(tile, slot_, r):
        d = dst_ref[(tile + 1) * tm + r]
        return pltpu.make_async_copy(obuf.at[slot_, pl.ds(r, 1)], y_hbm.at[pl.ds(d, 1)], ssem.at[slot_])

    def gather_tile(slot_):
        return pltpu.make_async_copy(h_hbm.at[pl.ds(0, tm)], xg.at[slot_], gsem.at[slot_])

    def scatter_tile(slot_):
        return pltpu.make_async_copy(obuf.at[slot_], y_hbm.at[pl.ds(0, tm)], ssem.at[slot_])

    def for_rows(fn):
        def body(r, _):
            fn(r)
            return 0
        lax.fori_loop(0, tm, body, 0, unroll=8)

    @pl.when(i == 0)
    def _():
        obuf[...] = jnp.zeros(obuf.shape, F32)
        n_pad = T_PAD - N_TOK
        for first, count in ((N_TOK, n_pad), (T_PAD + N_TOK, n_pad), (2 * T_PAD, MOE_TRASH_ROWS)):
            for off in range(0, count, tm):
                n = min(tm, count - off)
                cp = pltpu.make_async_copy(obuf.at[0, pl.ds(0, n)], y_hbm.at[pl.ds(first + off, n)], zsem)
                cp.start()
                cp.wait()
        for_rows(lambda r: gather_row(0, 0, r).start())

    gather_tile(slot).wait()

    @pl.when(i < nused_ref[0])
    def _():
        x = xg[slot].astype(BF16)
        acc = jnp.zeros((tm, D_MODEL), F32)
        for c in range(n_chunks):
            cols = slice(c * MOE_FC, (c + 1) * MOE_FC)
            a = (_silu(_dot(x, wg_ref[0, :, cols])) * _dot(x, wu_ref[0, :, cols])).astype(BF16)
            acc = acc + _dot(a, wd_ref[0, cols, :])
            for r in range(c * tm // n_chunks, (c + 1) * tm // n_chunks):
                gather_row(i + 1, other, r).start()
                scatter_row(i - 1, other, r).start()
        obuf[slot] = acc * gate_ref[...]

    @pl.when(i >= nused_ref[0])
    def _():
        for_rows(lambda r: gather_row(i + 1, other, r).start())
        for_rows(lambda r: scatter_row(i - 1, other, r).start())

    scatter_tile(other).wait()

    @pl.when(i == pl.num_programs(0) - 1)
    def _():
        for_rows(lambda r: scatter_row(i, slot, r).start())
        scatter_tile(slot).wait()
        gather_tile(other).wait()


def _moe_layer(h2, tile_expert, n_used, src, dst, gate_sorted, w_gate, w_up, w_down):
    tm = MOE_TM
    grid_spec = pltpu.PrefetchScalarGridSpec(
        num_scalar_prefetch=4,
        grid=(MOE_TILES,),
        in_specs=[
            pl.BlockSpec(memory_space=pl.ANY),
            pl.BlockSpec((tm, 1), lambda i, te, nu, s, d: (i, 0)),
            pl.BlockSpec((1, D_MODEL, D_FF_EXPERT), lambda i, te, nu, s, d: (te[i], 0, 0)),
            pl.BlockSpec((1, D_MODEL, D_FF_EXPERT), lambda i, te, nu, s, d: (te[i], 0, 0)),
            pl.BlockSpec((1, D_FF_EXPERT, D_MODEL), lambda i, te, nu, s, d: (te[i], 0, 0)),
        ],
        out_specs=pl.BlockSpec(memory_space=pl.ANY),
        scratch_shapes=[
            pltpu.VMEM((2, tm, D_MODEL), F32),
            pltpu.VMEM((2, tm, D_MODEL), F32),
            pltpu.SemaphoreType.DMA((2,)),
            pltpu.SemaphoreType.DMA((2,)),
            pltpu.SemaphoreType.DMA(()),
        ],
    )
    return pl.pallas_call(
        _moe_kernel,
        grid_spec=grid_spec,
        out_shape=jax.ShapeDtypeStruct((Y_ROWS, D_MODEL), F32),
        compiler_params=_params(),
        name="moe_experts",
    )(tile_expert, n_used, src, dst, h2, gate_sorted, w_gate, w_up, w_down)


def _moe_plan(route):
    tm = MOE_TM
    r = route[:N_TOK]
    expert = jnp.concatenate([r[:, 0], r[:, 1]]).astype(jnp.int32)
    gate = jnp.concatenate([r[:, 2], r[:, 3]])
    tok = jnp.arange(N_TOK, dtype=jnp.int32)
    flat = jnp.concatenate([tok, T_PAD + tok])
    onehot = (expert[:, None] == jnp.arange(N_EXPERTS, dtype=jnp.int32)[None, :]).astype(jnp.int32)
    csum = jnp.cumsum(onehot, axis=0)
    counts = csum[-1]
    rank = jnp.sum(csum * onehot, axis=1) - 1
    tiles = (counts + tm - 1) // tm
    tile_end = jnp.cumsum(tiles)
    tile_start = tile_end - tiles
    pos = tile_start[expert] * tm + rank
    n_used = tile_end[-1]
    table = jnp.full((MOE_ROWS, 2), -1, jnp.int32).at[pos].set(
        jnp.stack([flat, lax.bitcast_convert_type(gate, jnp.int32)], axis=1))
    dst = table[:, 0]
    is_pad = dst < 0
    gate_sorted = jnp.where(is_pad, 0.0, lax.bitcast_convert_type(table[:, 1], F32))
    src = jnp.where(is_pad, 0, dst % T_PAD)
    pad_rank = jnp.cumsum(is_pad.astype(jnp.int32)) - 1
    dst = jnp.where(is_pad, 2 * T_PAD + pad_rank, dst)
    src = jnp.concatenate([src, jnp.zeros((tm,), jnp.int32)])
    dst = jnp.concatenate([2 * T_PAD + MOE_PAD_ROWS + jnp.arange(tm, dtype=jnp.int32), dst])
    tile_id = jnp.arange(MOE_TILES, dtype=jnp.int32)
    texp = jnp.sum((tile_id[:, None] >= tile_end[None, :]).astype(jnp.int32), axis=1)
    last = jnp.sum((n_used - 1 >= tile_end).astype(jnp.int32))
    texp = jnp.minimum(jnp.where(tile_id < n_used, texp, last), N_EXPERTS - 1)
    return texp, n_used.reshape(1), src, dst, gate_sorted.reshape(MOE_ROWS, 1)


def _combine_kernel(x_ref, y0_ref, y1_ref, o_ref):
    o_ref[...] = x_ref[...] + y0_ref[...] + y1_ref[...]


def _combine_norm_kernel(x_ref, y0_ref, y1_ref, g_ref, o_ref):
    o_ref[...] = _rms_norm(x_ref[...] + y0_ref[...] + y1_ref[...], g_ref[...])


def _combine(x1, y, final_g=None):
    tb = TOKEN_BLOCK
    in_specs = [
        pl.BlockSpec((tb, D_MODEL), lambda i: (i, 0)),
        pl.BlockSpec((tb, D_MODEL), lambda i: (i, 0)),
        pl.BlockSpec((tb, D_MODEL), lambda i: (N_BLOCKS + i, 0)),
    ]
    args = [x1, y, y]
    if final_g is not None:
        in_specs.append(_resident((1, D_MODEL)))
        args.append(final_g)
    return pl.pallas_call(
        _combine_kernel if final_g is None else _combine_norm_kernel,
        grid=(N_BLOCKS,),
        in_specs=in_specs,
        out_specs=pl.BlockSpec((tb, D_MODEL), lambda i: (i, 0)),
        out_shape=jax.ShapeDtypeStruct((T_PAD, D_MODEL), F32),
        name="moe_combine",
    )(*args)


def _swap_halves(w):
    half = w.shape[-1] // 2
    return jnp.concatenate([w[..., half:], w[..., :half]], axis=-1)


def _pad_last(w, n):
    return jnp.pad(w, [(0, 0)] * (w.ndim - 1) + [(0, n - w.shape[-1])])


def _rope_tables():
    half = QK_ROPE // 2
    pos = jnp.concatenate([
        jnp.tile(jnp.arange(SEQ, dtype=jnp.int32), BATCH),
        jnp.full((N_SAMPLE,), PAST_LEN, jnp.int32),
        jnp.zeros((T_PAD - N_TOK,), jnp.int32),
    ])
    inv_freq = ROPE_THETA ** (-jnp.arange(half, dtype=F32) / half)
    ang = pos.astype(F32)[:, None] * inv_freq[None, :]
    cos, sin = jnp.cos(ang), jnp.sin(ang)
    return (jnp.concatenate([cos, cos, cos, cos], axis=-1),
            jnp.concatenate([-sin, sin, -sin, sin], axis=-1))


def _sgu_spatial_params(w_s, b_s):
    causal = jnp.tril(jnp.ones((CHUNK, CHUNK), dtype=bool))
    ws_causal = jnp.where(causal[None], w_s, 0.0).astype(BF16)
    bias_full = jnp.repeat(b_s.T, CHUNK, axis=1)
    ws0_row = jnp.repeat(w_s[:, 0, 0], CHUNK)[None, :]
    bias0_row = jnp.repeat(b_s[:, 0], CHUNK)[None, :]
    return ws_causal, bias_full, ws0_row, bias0_row


def _mla_params(w_dq, w_uq, w_dkv, w_uk, w_uv):
    qn = w_uq[:, :, :QK_NOPE].reshape(Q_LORA, N_HEADS * QK_NOPE)
    qr = w_uq[:, :, QK_NOPE:]
    qr_pad = _pad_last(qr, 128).reshape(Q_LORA, N_HEADS * 128)
    qrs_pad = _pad_last(_swap_halves(qr), 128).reshape(Q_LORA, N_HEADS * 128)
    w_q = jnp.concatenate([qn, qr_pad, qrs_pad], axis=1).astype(BF16)
    kr = w_dkv[:, KV_LORA:]
    w_dkv3 = jnp.concatenate(
        [w_dkv[:, :KV_LORA], _pad_last(kr, 128), _pad_last(_swap_halves(kr), 128)], axis=1).astype(BF16)
    w_ukv = jnp.concatenate([w_uk.reshape(KV_LORA, -1), w_uv.reshape(KV_LORA, -1)], axis=1).astype(BF16)
    w_ukt = w_uk.transpose(1, 2, 0).astype(BF16)
    w_uvh = w_uv.transpose(1, 0, 2).astype(BF16)
    return w_dq.astype(BF16), w_q, w_dkv3, w_ukv, w_ukt, w_uvh


def kernel(x_prompt, x_sample, cache_latent, cache_k_rope, page_table, norm_mix_g, norm_ffn_g, final_norm_g, sgu_w_in, sgu_ln_g, sgu_ln_b, sgu_w_s, sgu_b_s, sgu_w_out, mla_w_dq, mla_q_norm_g, mla_w_uq, mla_w_dkv, mla_kv_norm_g, mla_w_uk, mla_w_uv, mla_w_o, ffn_w_gate, ffn_w_up, ffn_w_down, moe_w_router, moe_w_gate, moe_w_up, moe_w_down):
    x = jnp.concatenate([
        x_prompt.reshape(N_PROMPT, D_MODEL),
        x_sample.reshape(N_SAMPLE, D_MODEL),
        jnp.zeros((T_PAD - N_TOK, D_MODEL), F32),
    ])
    cos_t, sin_t = _rope_tables()
    pt_flat = page_table.reshape(-1)
    kr_pool_t = jnp.swapaxes(cache_k_rope, 2, 3)
    lat_all, kr_all, v_s = [], [], []
    y = None
    for i in range(DEPTH):
        j = i // 2
        g_mix = norm_mix_g[i].reshape(1, D_MODEL)
        g_ffn = norm_ffn_g[i].reshape(1, D_MODEL)
        if i % 2 == 0:
            x, vs = _sgu_layer(x, g_mix, sgu_w_in[j], sgu_ln_g[j].reshape(1, -1),
                               sgu_ln_b[j].reshape(1, -1), *_sgu_spatial_params(sgu_w_s[j], sgu_b_s[j]),
                               sgu_w_out[j])
            v_s.append(vs.reshape(N_SAMPLE, 1, D_SGU))
            w_gu = jnp.concatenate([ffn_w_gate[j], ffn_w_up[j]], axis=1)
            x = _ffn_layer(x, g_ffn, w_gu, ffn_w_down[j])
        else:
            w_dq, w_q, w_dkv3, w_ukv, w_ukt, w_uvh = _mla_params(
                mla_w_dq[j], mla_w_uq[j], mla_w_dkv[j], mla_w_uk[j], mla_w_uv[j])
            q, k, v, lat, kr = _mla_proj(x, g_mix, w_dq, mla_q_norm_g[j].reshape(1, -1), w_q, w_dkv3,
                                         mla_kv_norm_g[j].reshape(1, -1), w_ukv, cos_t, sin_t)
            lat_all.append(lat)
            kr_all.append(kr)
            o_prompt = _flash_attention(q, k, v)
            qa = _sample_qabs(q, w_ukt)
            o_lat = _sample_attention(pt_flat, qa.transpose(1, 0, 2), lat, kr,
                                      cache_latent, kr_pool_t, layer=j)
            o_sample = _sample_oup(o_lat.transpose(1, 0, 2), w_uvh)
            w_r = _pad_last(moe_w_router[j], 128)
            x1, h2, route = _attn_out(x, o_prompt, o_sample, mla_w_o[j].astype(BF16), g_ffn, w_r)
            texp, n_used, src, dst, gate_sorted = _moe_plan(route)
            y = _moe_layer(h2, texp, n_used, src, dst, gate_sorted, moe_w_gate[j].astype(BF16),
                           moe_w_up[j].astype(BF16), moe_w_down[j].astype(BF16))
            final_g = final_norm_g.reshape(1, D_MODEL) if i == DEPTH - 1 else None
            x = _combine(x1, y, final_g)

    def split(a, width):
        return (a[:N_PROMPT].reshape(BATCH, SEQ, width), a[N_PROMPT:N_TOK].reshape(N_SAMPLE, 1, width))

    y_prompt, y_sample = split(x, D_MODEL)
    lat_p, lat_s = zip(*[split(a, KV_LORA) for a in lat_all])
    kr_p, kr_s = zip(*[split(a, QK_ROPE) for a in kr_all])
    return (y_prompt, y_sample, jnp.stack(lat_p), jnp.stack(kr_p), jnp.stack(lat_s), jnp.stack(kr_s),
            jnp.stack(v_s))
```

```python
import functools
import math

import jax
import jax.numpy as jnp
from jax import lax
from jax.experimental import pallas as pl
from jax.experimental.pallas import tpu as pltpu

F32 = jnp.float32
BF16 = jnp.bfloat16

D_MODEL = 1024
BATCH = 2
SEQ = 8192
DEPTH = 4
N_SAMPLE = 128
PAST_LEN = 8192
PAGE_SIZE = 128
N_PAGES = PAST_LEN // PAGE_SIZE

CHUNK = 128
D_SGU = 2 * D_MODEL
N_SGU_GROUPS = D_SGU // CHUNK

N_HEADS = 8
QK_NOPE = 128
QK_ROPE = 64
V_HEAD = 128
Q_LORA = 384
KV_LORA = 256
ROPE_THETA = 10000.0
ATTN_SCALE = 1.0 / math.sqrt(QK_NOPE + QK_ROPE)
QK_PAD = 256

D_FF = 5 * D_MODEL // 2
N_EXPERTS = 8
D_FF_EXPERT = 7 * D_MODEL // 2
NORM_EPS = 1e-6
LN_EPS = 1e-5
NEG_INF = -1e30

TOKEN_BLOCK = 512
N_PROMPT = BATCH * SEQ
N_TOK = N_PROMPT + N_SAMPLE
N_BLOCKS = pl.cdiv(N_TOK, TOKEN_BLOCK)
T_PAD = N_BLOCKS * TOKEN_BLOCK
SAMPLE_BLOCK = N_PROMPT // TOKEN_BLOCK
CHUNKS_PER_BLOCK = TOKEN_BLOCK // CHUNK

ATTN_TQ = 1024
ATTN_TK = 1024
ATTN_HEADS = 1

MOE_TM = 256
MOE_TILES = 2 * N_TOK // MOE_TM + N_EXPERTS
MOE_ROWS = MOE_TILES * MOE_TM
MOE_FC = 512
MOE_PAD_ROWS = MOE_ROWS - 2 * N_TOK
MOE_TRASH_ROWS = MOE_PAD_ROWS + MOE_TM
Y_ROWS = 2 * T_PAD + MOE_TRASH_ROWS

VMEM_LIMIT = 56 * 1024 * 1024


def _params(**kw):
    return pltpu.CompilerParams(vmem_limit_bytes=VMEM_LIMIT, **kw)


def _resident(shape):
    return pl.BlockSpec(shape, lambda *_: (0,) * len(shape), pipeline_mode=pl.Buffered(1))


def _rms_norm(x, g):
    return x * lax.rsqrt(jnp.mean(x * x, axis=-1, keepdims=True) + NORM_EPS) * g


def _gelu(x):
    return 0.5 * x * (1.0 + lax.erf(x * math.sqrt(0.5)))


def _silu(x):
    return x * (1.0 / (1.0 + jnp.exp(-x)))


def _dot(a, b):
    return jnp.dot(a, b, preferred_element_type=F32)


def _split_bf16(w):
    hi = w.astype(BF16)
    return hi, (w - hi.astype(F32)).astype(BF16)


def _split_weights_kernel(w_ref, hi_ref, lo_ref):
    hi_ref[...], lo_ref[...] = _split_bf16(w_ref[...])


def _split_weights(w):
    rows, cols = w.shape
    br = 256
    spec = pl.BlockSpec((br, cols), lambda i: (i, 0))
    return pl.pallas_call(
        _split_weights_kernel,
        grid=(rows // br,),
        in_specs=[spec],
        out_specs=[spec, spec],
        out_shape=[jax.ShapeDtypeStruct(w.shape, BF16)] * 2,
        name="split_weights",
    )(w)


def _dot_split(a, w_hi_ref, w_lo_ref):
    a_hi, a_lo = _split_bf16(a)
    w_hi = w_hi_ref[...]
    return _dot(a_hi, w_hi) + (_dot(a_lo, w_hi) + _dot(a_hi, w_lo_ref[...]))


def _dot_nt(a, b):
    return lax.dot_general(a, b, (((1,), (1,)), ((), ())), preferred_element_type=F32)


def _sgu_gelu_ln(z, lng_ref, lnb_ref):
    z = _gelu(z)
    u = z[:, :D_SGU]
    v = z[:, D_SGU:]
    vc = v - jnp.mean(v, axis=-1, keepdims=True)
    vn = vc * lax.rsqrt(jnp.mean(vc * vc, axis=-1, keepdims=True) + LN_EPS)
    return u, vn * lng_ref[...] + lnb_ref[...]


def _sgu_kernel(*refs, n_planes):
    x_refs, refs = refs[:n_planes], refs[n_planes:]
    (g_ref, win_ref, winlo_ref, lng_ref, lnb_ref, ws_ref, bias_ref, ws0_ref, bias0_ref,
     wout_ref, woutlo_ref, o_ref, v_ref, gated_ref) = refs
    is_sample = pl.program_id(0) == SAMPLE_BLOCK

    def read_x(rows):
        x = x_refs[0][rows, :]
        for r in x_refs[1:]:
            x = x + r[rows, :]
        return x

    @pl.when(is_sample)
    def _():
        x = read_x(slice(0, N_SAMPLE))
        h = _rms_norm(x, g_ref[...])
        u, vn = _sgu_gelu_ln(_dot_split(h, win_ref, winlo_ref), lng_ref, lnb_ref)
        v_ref[...] = vn
        gated = u * (vn * ws0_ref[...] + bias0_ref[...])
        o_ref[0:N_SAMPLE, :] = x + _dot_split(gated, wout_ref, woutlo_ref)
        o_ref[N_SAMPLE:, :] = jnp.zeros((TOKEN_BLOCK - N_SAMPLE, D_MODEL), F32)

    @pl.when(jnp.logical_not(is_sample))
    def _():
        x = read_x(slice(None))
        h = _rms_norm(x, g_ref[...]).astype(BF16)
        u, vn = _sgu_gelu_ln(_dot(h, win_ref[...]), lng_ref, lnb_ref)
        vb = vn.astype(BF16)
        for g in range(N_SGU_GROUPS):
            cols = slice(g * CHUNK, (g + 1) * CHUNK)
            rhs = jnp.concatenate(
                [vb[c * CHUNK:(c + 1) * CHUNK, cols] for c in range(CHUNKS_PER_BLOCK)], axis=1)
            mixed = _dot(ws_ref[g], rhs)
            for c in range(CHUNKS_PER_BLOCK):
                rows = slice(c * CHUNK, (c + 1) * CHUNK)
                m = mixed[:, c * CHUNK:(c + 1) * CHUNK] + bias_ref[:, cols]
                gated_ref[rows, cols] = (u[rows, cols] * m).astype(BF16)
        o_ref[...] = x + _dot(gated_ref[...], wout_ref[...])


def _sgu_layer(x, expert_out, norm_g, w_in, ln_g, ln_b, ws_causal, bias_full, ws0_row, bias0_row, w_out):
    w_in, w_in_lo = _split_weights(w_in)
    w_out, w_out_lo = _split_weights(w_out)
    tb = TOKEN_BLOCK
    x_specs = [pl.BlockSpec((tb, D_MODEL), lambda i: (i, 0))]
    x_args = [x]
    if expert_out is not None:
        x_specs += [pl.BlockSpec((tb, D_MODEL), lambda i: (i, 0)),
                    pl.BlockSpec((tb, D_MODEL), lambda i: (N_BLOCKS + i, 0))]
        x_args += [expert_out, expert_out]
    return pl.pallas_call(
        functools.partial(_sgu_kernel, n_planes=len(x_args)),
        grid=(N_BLOCKS,),
        in_specs=x_specs + [
            _resident((1, D_MODEL)),
            _resident((D_MODEL, 2 * D_SGU)),
            _resident((D_MODEL, 2 * D_SGU)),
            _resident((1, D_SGU)),
            _resident((1, D_SGU)),
            _resident((N_SGU_GROUPS, CHUNK, CHUNK)),
            _resident((CHUNK, D_SGU)),
            _resident((1, D_SGU)),
            _resident((1, D_SGU)),
            _resident((D_SGU, D_MODEL)),
            _resident((D_SGU, D_MODEL)),
        ],
        out_specs=[
            pl.BlockSpec((tb, D_MODEL), lambda i: (i, 0)),
            pl.BlockSpec((N_SAMPLE, D_SGU), lambda i: (0, 0)),
        ],
        out_shape=[
            jax.ShapeDtypeStruct((T_PAD, D_MODEL), F32),
            jax.ShapeDtypeStruct((N_SAMPLE, D_SGU), F32),
        ],
        scratch_shapes=[pltpu.VMEM((tb, D_SGU), BF16)],
        compiler_params=_params(),
        name="sgu_mixer",
    )(*x_args, norm_g, w_in, w_in_lo, ln_g, ln_b, ws_causal, bias_full, ws0_row, bias0_row, w_out, w_out_lo)


def _ffn_kernel(x_ref, g_ref, wgu_ref, wgulo_ref, wd_ref, wdlo_ref, o_ref):
    is_sample = pl.program_id(0) == SAMPLE_BLOCK

    @pl.when(is_sample)
    def _():
        x = x_ref[0:N_SAMPLE, :]
        gu = _dot_split(_rms_norm(x, g_ref[...]), wgu_ref, wgulo_ref)
        a = _silu(gu[:, :D_FF]) * gu[:, D_FF:]
        o_ref[0:N_SAMPLE, :] = x + _dot_split(a, wd_ref, wdlo_ref)
        o_ref[N_SAMPLE:, :] = jnp.zeros((TOKEN_BLOCK - N_SAMPLE, D_MODEL), F32)

    @pl.when(jnp.logical_not(is_sample))
    def _():
        x = x_ref[...]
        h = _rms_norm(x, g_ref[...]).astype(BF16)
        gu = _dot(h, wgu_ref[...])
        a = (_silu(gu[:, :D_FF]) * gu[:, D_FF:]).astype(BF16)
        o_ref[...] = x + _dot(a, wd_ref[...])


def _ffn_layer(x, norm_g, w_gu, w_down):
    tb = TOKEN_BLOCK
    w_gu, w_gu_lo = _split_weights(w_gu)
    w_down, w_down_lo = _split_weights(w_down)
    return pl.pallas_call(
        _ffn_kernel,
        grid=(N_BLOCKS,),
        in_specs=[
            pl.BlockSpec((tb, D_MODEL), lambda i: (i, 0)),
            _resident((1, D_MODEL)),
            _resident((D_MODEL, 2 * D_FF)),
            _resident((D_MODEL, 2 * D_FF)),
            _resident((D_FF, D_MODEL)),
            _resident((D_FF, D_MODEL)),
        ],
        out_specs=pl.BlockSpec((tb, D_MODEL), lambda i: (i, 0)),
        out_shape=jax.ShapeDtypeStruct((T_PAD, D_MODEL), F32),
        compiler_params=_params(),
        name="dense_ffn",
    )(x, norm_g, w_gu, w_gu_lo, w_down, w_down_lo)


def _mla_proj_kernel(x_ref, g_ref, wdq_ref, qg_ref, wq_ref, wdkv_ref, kvg_ref, wukv_ref,
                     cos_ref, sin_ref, q_ref, k_ref, v_ref, lat_ref, kr_ref):
    hd_w = N_HEADS * QK_NOPE
    x = x_ref[...]
    h = _rms_norm(x, g_ref[...]).astype(BF16)
    cq = _rms_norm(_dot(h, wdq_ref[...]), qg_ref[...]).astype(BF16)
    q = _dot(cq, wq_ref[...])
    cos = cos_ref[...]
    sin = sin_ref[...]
    ckv = _dot(h, wdkv_ref[...])
    c = _rms_norm(ckv[:, :KV_LORA], kvg_ref[...])
    lat_ref[...] = c
    kr = ckv[:, KV_LORA:KV_LORA + 128] * cos + ckv[:, KV_LORA + 128:] * sin
    kr_ref[...] = kr[:, :QK_ROPE]
    krb = kr.astype(BF16)
    kv = _dot(c.astype(BF16), wukv_ref[...])
    for hd in range(N_HEADS):
        lo, hi = hd * 128, (hd + 1) * 128
        qr = q[:, hd_w + lo:hd_w + hi] * cos + q[:, 2 * hd_w + lo:2 * hd_w + hi] * sin
        q_ref[hd, :, 0:128] = q[:, lo:hi].astype(BF16)
        q_ref[hd, :, 128:256] = qr.astype(BF16)
        k_ref[hd, :, 0:128] = kv[:, lo:hi].astype(BF16)
        k_ref[hd, :, 128:256] = krb
        v_ref[hd] = kv[:, hd_w + lo:hd_w + hi].astype(BF16)


def _mla_proj(x, norm_g, w_dq, q_norm_g, w_q, w_dkv, kv_norm_g, w_ukv, cos_t, sin_t):
    tb = TOKEN_BLOCK
    return pl.pallas_call(
        _mla_proj_kernel,
        grid=(N_BLOCKS,),
        in_specs=[
            pl.BlockSpec((tb, D_MODEL), lambda i: (i, 0)),
            _resident((1, D_MODEL)),
            _resident((D_MODEL, Q_LORA)),
            _resident((1, Q_LORA)),
            _resident((Q_LORA, 3 * N_HEADS * 128)),
            _resident((D_MODEL, KV_LORA + 256)),
            _resident((1, KV_LORA)),
            _resident((KV_LORA, 2 * N_HEADS * 128)),
            pl.BlockSpec((tb, 128), lambda i: (i, 0)),
            pl.BlockSpec((tb, 128), lambda i: (i, 0)),
        ],
        out_specs=[
            pl.BlockSpec((N_HEADS, tb, QK_PAD), lambda i: (0, i, 0)),
            pl.BlockSpec((N_HEADS, tb, QK_PAD), lambda i: (0, i, 0)),
            pl.BlockSpec((N_HEADS, tb, V_HEAD), lambda i: (0, i, 0)),
            pl.BlockSpec((tb, KV_LORA), lambda i: (i, 0)),
            pl.BlockSpec((tb, QK_ROPE), lambda i: (i, 0)),
        ],
        out_shape=[
            jax.ShapeDtypeStruct((N_HEADS, T_PAD, QK_PAD), BF16),
            jax.ShapeDtypeStruct((N_HEADS, T_PAD, QK_PAD), BF16),
            jax.ShapeDtypeStruct((N_HEADS, T_PAD, V_HEAD), BF16),
            jax.ShapeDtypeStruct((T_PAD, KV_LORA), F32),
            jax.ShapeDtypeStruct((T_PAD, QK_ROPE), F32),
        ],
        compiler_params=_params(),
        name="mla_proj",
    )(x, norm_g, w_dq, q_norm_g, w_q, w_dkv, kv_norm_g, w_ukv, cos_t, sin_t)


def _flash_kernel(q_ref, k_ref, v_ref, o_ref):
    tq, tk = ATTN_TQ, ATTN_TK
    qi = pl.program_id(2)
    c = ATTN_SCALE * math.log2(math.e)

    def step(j, carry, masked):
        start = pl.multiple_of(j * tk, tk)
        out = []
        for hd in range(ATTN_HEADS):
            m, l, acc = carry[hd]
            k = k_ref[hd, pl.ds(start, tk), :]
            v = v_ref[hd, pl.ds(start, tk), :]
            s = _dot_nt(q_ref[hd], k)
            if masked:
                row = lax.broadcasted_iota(jnp.int32, (tq, tk), 0)
                col = lax.broadcasted_iota(jnp.int32, (tq, tk), 1)
                s = jnp.where(col <= row, s, NEG_INF)
            m_new = jnp.maximum(m, jnp.max(s, axis=-1, keepdims=True))
            mc = m_new * c
            alpha = jnp.exp2(m * c - mc)
            p = jnp.exp2(s * c - mc)
            l = alpha * l + jnp.sum(p, axis=-1, keepdims=True)
            acc = alpha * acc + _dot(p.astype(BF16), v)
            out.append((m_new, l, acc))
        return tuple(out)

    init = tuple((jnp.full((tq, 1), NEG_INF, F32), jnp.zeros((tq, 1), F32),
                  jnp.zeros((tq, V_HEAD), F32)) for _ in range(ATTN_HEADS))
    carry = lax.fori_loop(0, qi, functools.partial(step, masked=False), init)
    carry = step(qi, carry, masked=True)
    for hd in range(ATTN_HEADS):
        _, l, acc = carry[hd]
        o_ref[:, hd * V_HEAD:(hd + 1) * V_HEAD] = (acc * (1.0 / l)).astype(BF16)


def _flash_attention(q, k, v):
    assert ATTN_TQ == ATTN_TK
    nq = SEQ // ATTN_TQ
    g = ATTN_HEADS
    return pl.pallas_call(
        _flash_kernel,
        grid=(BATCH, N_HEADS // g, nq),
        in_specs=[
            pl.BlockSpec((g, ATTN_TQ, QK_PAD), lambda b, h, i: (h, b * nq + i, 0)),
            pl.BlockSpec((g, SEQ, QK_PAD), lambda b, h, i: (h, b, 0)),
            pl.BlockSpec((g, SEQ, V_HEAD), lambda b, h, i: (h, b, 0)),
        ],
        out_specs=pl.BlockSpec((ATTN_TQ, g * V_HEAD), lambda b, h, i: (b * nq + i, h)),
        out_shape=jax.ShapeDtypeStruct((N_PROMPT, N_HEADS * V_HEAD), BF16),
        compiler_params=_params(),
        name="flash_attention",
    )(q, k, v)


def _sample_qabs_kernel(q_ref, wukt_ref, qa_ref):
    for hd in range(N_HEADS):
        q = q_ref[hd]
        qa_ref[hd, :, 0:KV_LORA] = _dot(q[:, :QK_NOPE], wukt_ref[hd])
        qa_ref[hd, :, KV_LORA:] = q[:, QK_NOPE:].astype(F32)


def _sample_qabs(q, w_ukt):
    blk = N_PROMPT // N_SAMPLE
    return pl.pallas_call(
        _sample_qabs_kernel,
        grid=(1,),
        in_specs=[
            pl.BlockSpec((N_HEADS, N_SAMPLE, QK_PAD), lambda i: (0, blk, 0)),
            pl.BlockSpec((N_HEADS, QK_NOPE, KV_LORA), lambda i: (0, 0, 0)),
        ],
        out_specs=pl.BlockSpec((N_HEADS, N_SAMPLE, KV_LORA + 128), lambda i: (0, 0, 0)),
        out_shape=jax.ShapeDtypeStruct((N_HEADS, N_SAMPLE, KV_LORA + 128), F32),
        name="sample_q_absorb",
    )(q, w_ukt)


def _page_copies(layer, pt_ref, lat_hbm, kr_hbm, lat_buf, kr_buf, sem, seq, slot, page):
    pid = pt_ref[seq * N_PAGES + page]
    rows = pl.ds(pl.multiple_of(page * PAGE_SIZE, PAGE_SIZE), PAGE_SIZE)
    return (
        pltpu.make_async_copy(lat_hbm.at[layer, pid], lat_buf.at[slot, rows], sem.at[0, slot]),
        pltpu.make_async_copy(kr_hbm.at[layer, pid], kr_buf.at[slot, :, rows], sem.at[1, slot]),
    )


def _sample_attn_kernel(pt_ref, qa_ref, cn_ref, krn_ref, lat_hbm, kr_hbm, o_ref,
                        lat_buf, kr_buf, sem, *, layer):
    n = pl.program_id(0)
    slot = lax.rem(n, 2)
    copies = functools.partial(_page_copies, layer, pt_ref, lat_hbm, kr_hbm, lat_buf, kr_buf, sem)

    def start_pages(seq, slot_):
        def body(page, _):
            for cp in copies(seq, slot_, page):
                cp.start()
            return 0
        lax.fori_loop(0, N_PAGES, body, 0)

    @pl.when(n == 0)
    def _():
        start_pages(0, 0)

    @pl.when(n + 1 < pl.num_programs(0))
    def _():
        start_pages(n + 1, 1 - slot)

    def wait_body(page, _):
        for cp in copies(n, slot, page):
            cp.wait()
        return 0
    lax.fori_loop(0, N_PAGES, wait_body, 0)

    qa = qa_ref[0]
    ql = qa[:, :KV_LORA].astype(BF16)
    qr = qa[:, KV_LORA:KV_LORA + QK_ROPE].astype(BF16)
    lat = lat_buf[slot].astype(BF16)
    kr_t = kr_buf[slot].astype(BF16)
    s = (_dot_nt(ql, lat) + _dot(qr, kr_t)) * ATTN_SCALE
    cn = cn_ref[pl.ds(n, 1), :]
    krn = krn_ref[pl.ds(n, 1), :]
    s_new = (jnp.sum(ql.astype(F32) * cn.astype(BF16).astype(F32), axis=-1, keepdims=True)
             + jnp.sum(qr.astype(F32) * krn.astype(BF16).astype(F32), axis=-1, keepdims=True)) * ATTN_SCALE
    m = jnp.maximum(jnp.max(s, axis=-1, keepdims=True), s_new)
    p = jnp.exp(s - m)
    p_new = jnp.exp(s_new - m)
    inv_l = 1.0 / (jnp.sum(p, axis=-1, keepdims=True) + p_new)
    o_ref[0] = _dot((p * inv_l).astype(BF16), lat) + (p_new * inv_l) * cn


def _sample_attention(page_table, qa_t, lat, kr, lat_pool, kr_pool, layer):
    blk = N_PROMPT // N_SAMPLE
    grid_spec = pltpu.PrefetchScalarGridSpec(
        num_scalar_prefetch=1,
        grid=(N_SAMPLE,),
        in_specs=[
            pl.BlockSpec((1, N_HEADS, KV_LORA + 128), lambda n, pt: (n, 0, 0)),
            pl.BlockSpec((N_SAMPLE, KV_LORA), lambda n, pt: (blk, 0)),
            pl.BlockSpec((N_SAMPLE, QK_ROPE), lambda n, pt: (blk, 0)),
            pl.BlockSpec(memory_space=pl.ANY),
            pl.BlockSpec(memory_space=pl.ANY),
        ],
        out_specs=pl.BlockSpec((1, N_HEADS, KV_LORA), lambda n, pt: (n, 0, 0)),
        scratch_shapes=[
            pltpu.VMEM((2, PAST_LEN, KV_LORA), F32),
            pltpu.VMEM((2, QK_ROPE, PAST_LEN), F32),
            pltpu.SemaphoreType.DMA((2, 2)),
        ],
    )
    return pl.pallas_call(
        functools.partial(_sample_attn_kernel, layer=layer),
        grid_spec=grid_spec,
        out_shape=jax.ShapeDtypeStruct((N_SAMPLE, N_HEADS, KV_LORA), F32),
        compiler_params=_params(),
        name="sample_attention",
    )(page_table, qa_t, lat, kr, lat_pool, kr_pool)


def _sample_oup_kernel(ol_ref, wuv_ref, o_ref):
    o_ref[...] = jnp.zeros(o_ref.shape, BF16)
    for hd in range(N_HEADS):
        o = _dot(ol_ref[hd].astype(BF16), wuv_ref[hd])
        o_ref[0:N_SAMPLE, hd * V_HEAD:(hd + 1) * V_HEAD] = o.astype(BF16)


def _sample_oup(ol_t, w_uvh):
    return pl.pallas_call(
        _sample_oup_kernel,
        grid=(1,),
        in_specs=[
            pl.BlockSpec((N_HEADS, N_SAMPLE, KV_LORA), lambda i: (0, 0, 0)),
            pl.BlockSpec((N_HEADS, KV_LORA, V_HEAD), lambda i: (0, 0, 0)),
        ],
        out_specs=pl.BlockSpec((TOKEN_BLOCK, N_HEADS * V_HEAD), lambda i: (0, 0)),
        out_shape=jax.ShapeDtypeStruct((TOKEN_BLOCK, N_HEADS * V_HEAD), BF16),
        name="sample_o_up",
    )(ol_t, w_uvh)


def _attn_out_kernel(x_ref, op_ref, os_ref, wo_ref, g_ref, wr_ref, wrlo_ref, x1_ref, h2_ref, route_ref):
    tb = TOKEN_BLOCK
    is_sample = pl.program_id(0) == SAMPLE_BLOCK
    o = jnp.where(is_sample, os_ref[...], op_ref[...])
    x1 = x_ref[...] + _dot(o, wo_ref[...])
    x1_ref[...] = x1
    h2 = _rms_norm(x1, g_ref[...])
    h2_ref[...] = h2
    logits = _dot_split(h2, wr_ref, wrlo_ref)
    lane = lax.broadcasted_iota(jnp.int32, (tb, 128), 1)
    lane_f = lane.astype(F32)
    lg = jnp.where(lane < N_EXPERTS, logits, -jnp.inf)
    v1 = jnp.max(lg, axis=-1, keepdims=True)
    i1 = jnp.min(jnp.where(lg == v1, lane_f, 128.0), axis=-1, keepdims=True)
    lg2 = jnp.where(lane_f == i1, -jnp.inf, lg)
    v2 = jnp.max(lg2, axis=-1, keepdims=True)
    i2 = jnp.min(jnp.where(lg2 == v2, lane_f, 128.0), axis=-1, keepdims=True)
    e2 = jnp.exp(v2 - v1)
    den = 1.0 + e2
    g1 = 1.0 / den
    g2 = e2 / den
    route_ref[...] = jnp.where(lane == 0, i1, jnp.where(lane == 1, i2,
                               jnp.where(lane == 2, g1, jnp.where(lane == 3, g2, 0.0))))


def _attn_out(x, o_prompt, o_sample, w_o, norm_g, w_router):
    tb = TOKEN_BLOCK
    return pl.pallas_call(
        _attn_out_kernel,
        grid=(N_BLOCKS,),
        in_specs=[
            pl.BlockSpec((tb, D_MODEL), lambda i: (i, 0)),
            pl.BlockSpec((tb, D_MODEL), lambda i: (jnp.minimum(i, SAMPLE_BLOCK - 1), 0)),
            pl.BlockSpec((tb, D_MODEL), lambda i: (0, 0)),
            _resident((N_HEADS * V_HEAD, D_MODEL)),
            _resident((1, D_MODEL)),
            _resident((D_MODEL, 128)),
            _resident((D_MODEL, 128)),
        ],
        out_specs=[
            pl.BlockSpec((tb, D_MODEL), lambda i: (i, 0)),
            pl.BlockSpec((tb, D_MODEL), lambda i: (i, 0)),
            pl.BlockSpec((tb, 128), lambda i: (i, 0)),
        ],
        out_shape=[
            jax.ShapeDtypeStruct((T_PAD, D_MODEL), F32),
            jax.ShapeDtypeStruct((T_PAD, D_MODEL), F32),
            jax.ShapeDtypeStruct((T_PAD, 128), F32),
        ],
        compiler_params=_params(),
        name="attn_out_router",
    )(x, o_prompt, o_sample, w_o, norm_g, *_split_weights(w_router))


def _moe_kernel(texp_ref, nused_ref, src_ref, dst_ref,
                h_hbm, gate_ref, wg_ref, wu_ref, wd_ref, y_hbm,
                xg, obuf, gsem, ssem, zsem):
    tm = MOE_TM
    n_chunks = D_FF_EXPERT // MOE_FC
    i = pl.program_id(0)
    slot = lax.rem(i, 2)
    other = 1 - slot

    def gather_row(tile, slot_, r):
        idx = src_ref[tile * tm + r]
        return pltpu.make_async_copy(h_hbm.at[pl.ds(idx, 1)], xg.at[slot_, pl.ds(r, 1)], gsem.at[slot_])

    def scatter_row(tile, slot_, r):
        d = dst_ref[(tile + 1) * tm + r]
        return pltpu.make_async_copy(obuf.at[slot_, pl.ds(r, 1)], y_hbm.at[pl.ds(d, 1)], ssem.at[slot_])

    def gather_tile(slot_):
        return pltpu.make_async_copy(h_hbm.at[pl.ds(0, tm)], xg.at[slot_], gsem.at[slot_])

    def scatter_tile(slot_):
        return pltpu.make_async_copy(obuf.at[slot_], y_hbm.at[pl.ds(0, tm)], ssem.at[slot_])

    def for_rows(fn):
        def body(r, _):
            fn(r)
            return 0
        lax.fori_loop(0, tm, body, 0, unroll=8)

    @pl.when(i == 0)
    def _():
        obuf[...] = jnp.zeros(obuf.shape, F32)
        n_pad = T_PAD - N_TOK
        for first, count in ((N_TOK, n_pad), (T_PAD + N_TOK, n_pad), (2 * T_PAD, MOE_TRASH_ROWS)):
            for off in range(0, count, tm):
                n = min(tm, count - off)
                cp = pltpu.make_async_copy(obuf.at[0, pl.ds(0, n)], y_hbm.at[pl.ds(first + off, n)], zsem)
                cp.start()
                cp.wait()
        for_rows(lambda r: gather_row(0, 0, r).start())

    gather_tile(slot).wait()

    @pl.when(i < nused_ref[0])
    def _():
        x = xg[slot].astype(BF16)
        acc = jnp.zeros((tm, D_MODEL), F32)
        for c in range(n_chunks):
            cols = slice(c * MOE_FC, (c + 1) * MOE_FC)
            a = (_silu(_dot(x, wg_ref[0, :, cols])) * _dot(x, wu_ref[0, :, cols])).astype(BF16)
            acc = acc + _dot(a, wd_ref[0, cols, :])
            for r in range(c * tm // n_chunks, (c + 1) * tm // n_chunks):
                gather_row(i + 1, other, r).start()
                scatter_row(i - 1, other, r).start()
        obuf[slot] = acc * gate_ref[...]

    @pl.when(i >= nused_ref[0])
    def _():
        for_rows(lambda r: gather_row(i + 1, other, r).start())
        for_rows(lambda r: scatter_row(i - 1, other, r).start())

    scatter_tile(other).wait()

    @pl.when(i == pl.num_programs(0) - 1)
    def _():
        for_rows(lambda r: scatter_row(i, slot, r).start())
        scatter_tile(slot).wait()
        gather_tile(other).wait()


def _moe_layer(h2, tile_expert, n_used, src, dst, gate_sorted, w_gate, w_up, w_down):
    tm = MOE_TM
    grid_spec = pltpu.PrefetchScalarGridSpec(
        num_scalar_prefetch=4,
        grid=(MOE_TILES,),
        in_specs=[
            pl.BlockSpec(memory_space=pl.ANY),
            pl.BlockSpec((tm, 1), lambda i, te, nu, s, d: (i, 0)),
            pl.BlockSpec((1, D_MODEL, D_FF_EXPERT), lambda i, te, nu, s, d: (te[i], 0, 0)),
            pl.BlockSpec((1, D_MODEL, D_FF_EXPERT), lambda i, te, nu, s, d: (te[i], 0, 0)),
            pl.BlockSpec((1, D_FF_EXPERT, D_MODEL), lambda i, te, nu, s, d: (te[i], 0, 0)),
        ],
        out_specs=pl.BlockSpec(memory_space=pl.ANY),
        scratch_shapes=[
            pltpu.VMEM((2, tm, D_MODEL), F32),
            pltpu.VMEM((2, tm, D_MODEL), F32),
            pltpu.SemaphoreType.DMA((2,)),
            pltpu.SemaphoreType.DMA((2,)),
            pltpu.SemaphoreType.DMA(()),
        ],
    )
    return pl.pallas_call(
        _moe_kernel,
        grid_spec=grid_spec,
        out_shape=jax.ShapeDtypeStruct((Y_ROWS, D_MODEL), F32),
        compiler_params=_params(),
        name="moe_experts",
    )(tile_expert, n_used, src, dst, h2, gate_sorted, w_gate, w_up, w_down)


def _moe_plan(route):
    tm = MOE_TM
    r = route[:N_TOK]
    expert = jnp.concatenate([r[:, 0], r[:, 1]]).astype(jnp.int32)
    gate = jnp.concatenate([r[:, 2], r[:, 3]])
    tok = jnp.arange(N_TOK, dtype=jnp.int32)
    flat = jnp.concatenate([tok, T_PAD + tok])
    onehot = (expert[:, None] == jnp.arange(N_EXPERTS, dtype=jnp.int32)[None, :]).astype(jnp.int32)
    csum = jnp.cumsum(onehot, axis=0)
    counts = csum[-1]
    rank = jnp.sum(csum * onehot, axis=1) - 1
    tiles = (counts + tm - 1) // tm
    tile_end = jnp.cumsum(tiles)
    tile_start = tile_end - tiles
    pos = tile_start[expert] * tm + rank
    n_used = tile_end[-1]
    table = jnp.full((MOE_ROWS, 2), -1, jnp.int32).at[pos].set(
        jnp.stack([flat, lax.bitcast_convert_type(gate, jnp.int32)], axis=1))
    dst = table[:, 0]
    is_pad = dst < 0
    gate_sorted = jnp.where(is_pad, 0.0, lax.bitcast_convert_type(table[:, 1], F32))
    src = jnp.where(is_pad, 0, dst % T_PAD)
    pad_rank = jnp.cumsum(is_pad.astype(jnp.int32)) - 1
    dst = jnp.where(is_pad, 2 * T_PAD + pad_rank, dst)
    src = jnp.concatenate([src, jnp.zeros((tm,), jnp.int32)])
    dst = jnp.concatenate([2 * T_PAD + MOE_PAD_ROWS + jnp.arange(tm, dtype=jnp.int32), dst])
    tile_id = jnp.arange(MOE_TILES, dtype=jnp.int32)
    texp = jnp.sum((tile_id[:, None] >= tile_end[None, :]).astype(jnp.int32), axis=1)
    last = jnp.sum((n_used - 1 >= tile_end).astype(jnp.int32))
    texp = jnp.minimum(jnp.where(tile_id < n_used, texp, last), N_EXPERTS - 1)
    return texp, n_used.reshape(1), src, dst, gate_sorted.reshape(MOE_ROWS, 1)


def _combine_kernel(x_ref, y0_ref, y1_ref, o_ref):
    o_ref[...] = x_ref[...] + y0_ref[...] + y1_ref[...]


def _combine_norm_kernel(x_ref, y0_ref, y1_ref, g_ref, o_ref):
    o_ref[...] = _rms_norm(x_ref[...] + y0_ref[...] + y1_ref[...], g_ref[...])


def _combine(x1, y, final_g=None):
    tb = TOKEN_BLOCK
    in_specs = [
        pl.BlockSpec((tb, D_MODEL), lambda i: (i, 0)),
        pl.BlockSpec((tb, D_MODEL), lambda i: (i, 0)),
        pl.BlockSpec((tb, D_MODEL), lambda i: (N_BLOCKS + i, 0)),
    ]
    args = [x1, y, y]
    if final_g is not None:
        in_specs.append(_resident((1, D_MODEL)))
        args.append(final_g)
    return pl.pallas_call(
        _combine_kernel if final_g is None else _combine_norm_kernel,
        grid=(N_BLOCKS,),
        in_specs=in_specs,
        out_specs=pl.BlockSpec((tb, D_MODEL), lambda i: (i, 0)),
        out_shape=jax.ShapeDtypeStruct((T_PAD, D_MODEL), F32),
        name="moe_combine",
    )(*args)


def _swap_halves(w):
    half = w.shape[-1] // 2
    return jnp.concatenate([w[..., half:], w[..., :half]], axis=-1)


def _pad_last(w, n):
    return jnp.pad(w, [(0, 0)] * (w.ndim - 1) + [(0, n - w.shape[-1])])


def _rope_tables():
    half = QK_ROPE // 2
    pos = jnp.concatenate([
        jnp.tile(jnp.arange(SEQ, dtype=jnp.int32), BATCH),
        jnp.full((N_SAMPLE,), PAST_LEN, jnp.int32),
        jnp.zeros((T_PAD - N_TOK,), jnp.int32),
    ])
    inv_freq = ROPE_THETA ** (-jnp.arange(half, dtype=F32) / half)
    ang = pos.astype(F32)[:, None] * inv_freq[None, :]
    cos, sin = jnp.cos(ang), jnp.sin(ang)
    return (jnp.concatenate([cos, cos, cos, cos], axis=-1),
            jnp.concatenate([-sin, sin, -sin, sin], axis=-1))


def _sgu_spatial_params(w_s, b_s):
    causal = jnp.tril(jnp.ones((CHUNK, CHUNK), dtype=bool))
    ws_causal = jnp.where(causal[None], w_s, 0.0).astype(BF16)
    bias_full = jnp.repeat(b_s.T, CHUNK, axis=1)
    ws0_row = jnp.repeat(w_s[:, 0, 0], CHUNK)[None, :]
    bias0_row = jnp.repeat(b_s[:, 0], CHUNK)[None, :]
    return ws_causal, bias_full, ws0_row, bias0_row


def _mla_params(w_dq, w_uq, w_dkv, w_uk, w_uv):
    qn = w_uq[:, :, :QK_NOPE].reshape(Q_LORA, N_HEADS * QK_NOPE)
    qr = w_uq[:, :, QK_NOPE:]
    qr_pad = _pad_last(qr, 128).reshape(Q_LORA, N_HEADS * 128)
    qrs_pad = _pad_last(_swap_halves(qr), 128).reshape(Q_LORA, N_HEADS * 128)
    w_q = jnp.concatenate([qn, qr_pad, qrs_pad], axis=1).astype(BF16)
    kr = w_dkv[:, KV_LORA:]
    w_dkv3 = jnp.concatenate(
        [w_dkv[:, :KV_LORA], _pad_last(kr, 128), _pad_last(_swap_halves(kr), 128)], axis=1).astype(BF16)
    w_ukv = jnp.concatenate([w_uk.reshape(KV_LORA, -1), w_uv.reshape(KV_LORA, -1)], axis=1).astype(BF16)
    w_ukt = w_uk.transpose(1, 2, 0).astype(BF16)
    w_uvh = w_uv.transpose(1, 0, 2).astype(BF16)
    return w_dq.astype(BF16), w_q, w_dkv3, w_ukv, w_ukt, w_uvh


def kernel(x_prompt, x_sample, cache_latent, cache_k_rope, page_table, norm_mix_g, norm_ffn_g, final_norm_g, sgu_w_in, sgu_ln_g, sgu_ln_b, sgu_w_s, sgu_b_s, sgu_w_out, mla_w_dq, mla_q_norm_g, mla_w_uq, mla_w_dkv, mla_kv_norm_g, mla_w_uk, mla_w_uv, mla_w_o, ffn_w_gate, ffn_w_up, ffn_w_down, moe_w_router, moe_w_gate, moe_w_up, moe_w_down):
    x = jnp.concatenate([
        x_prompt.reshape(N_PROMPT, D_MODEL),
        x_sample.reshape(N_SAMPLE, D_MODEL),
        jnp.zeros((T_PAD - N_TOK, D_MODEL), F32),
    ])
    cos_t, sin_t = _rope_tables()
    pt_flat = page_table.reshape(-1)
    kr_pool_t = jnp.swapaxes(cache_k_rope, 2, 3)
    lat_all, kr_all, v_s = [], [], []
    y = None
    for i in range(DEPTH):
        j = i // 2
        g_mix = norm_mix_g[i].reshape(1, D_MODEL)
        g_ffn = norm_ffn_g[i].reshape(1, D_MODEL)
        if i % 2 == 0:
            x, vs = _sgu_layer(x, y if i > 0 else None, g_mix, sgu_w_in[j], sgu_ln_g[j].reshape(1, -1),
                               sgu_ln_b[j].reshape(1, -1), *_sgu_spatial_params(sgu_w_s[j], sgu_b_s[j]),
                               sgu_w_out[j])
            v_s.append(vs.reshape(N_SAMPLE, 1, D_SGU))
            w_gu = jnp.concatenate([ffn_w_gate[j], ffn_w_up[j]], axis=1)
            x = _ffn_layer(x, g_ffn, w_gu, ffn_w_down[j])
        else:
            w_dq, w_q, w_dkv3, w_ukv, w_ukt, w_uvh = _mla_params(
                mla_w_dq[j], mla_w_uq[j], mla_w_dkv[j], mla_w_uk[j], mla_w_uv[j])
            q, k, v, lat, kr = _mla_proj(x, g_mix, w_dq, mla_q_norm_g[j].reshape(1, -1), w_q, w_dkv3,
                                         mla_kv_norm_g[j].reshape(1, -1), w_ukv, cos_t, sin_t)
            lat_all.append(lat)
            kr_all.append(kr)
            o_prompt = _flash_attention(q, k, v)
            qa = _sample_qabs(q, w_ukt)
            o_lat = _sample_attention(pt_flat, qa.transpose(1, 0, 2), lat, kr,
                                      cache_latent, kr_pool_t, layer=j)
            o_sample = _sample_oup(o_lat.transpose(1, 0, 2), w_uvh)
            w_r = _pad_last(moe_w_router[j], 128)
            x1, h2, route = _attn_out(x, o_prompt, o_sample, mla_w_o[j].astype(BF16), g_ffn, w_r)
            texp, n_used, src, dst, gate_sorted = _moe_plan(route)
            y = _moe_layer(h2, texp, n_used, src, dst, gate_sorted, moe_w_gate[j].astype(BF16),
                           moe_w_up[j].astype(BF16), moe_w_down[j].astype(BF16))
            if i == DEPTH - 1:
                x = _combine(x1, y, final_norm_g.reshape(1, D_MODEL))
            else:
                x = x1

    def split(a, width):
        return (a[:N_PROMPT].reshape(BATCH, SEQ, width), a[N_PROMPT:N_TOK].reshape(N_SAMPLE, 1, width))

    y_prompt, y_sample = split(x, D_MODEL)
    lat_p, lat_s = zip(*[split(a, KV_LORA) for a in lat_all])
    kr_p, kr_s = zip(*[split(a, QK_ROPE) for a in kr_all])
    return (y_prompt, y_sample, jnp.stack(lat_p), jnp.stack(kr_p), jnp.stack(lat_s), jnp.stack(kr_s),
            jnp.stack(v_s))
```
